```python
import math
import jax
import jax.numpy as jnp
from jax import lax
import numpy as np


D_MODEL = 1024
BATCH = 4
SEQ = 4096
DEPTH = 2
DEC_BATCH = 16
DEC_SEQ = 2048
PAST_LEN = 128

HEAD_DIM = D_MODEL // 16
DN_HEADS = 4
RET_HEADS = 4
ATT_HEADS = 8
DN_WIDTH = DN_HEADS * HEAD_DIM
RET_WIDTH = RET_HEADS * HEAD_DIM
ATT_WIDTH = ATT_HEADS * HEAD_DIM
MIX_WIDTH = DN_WIDTH + RET_WIDTH + ATT_WIDTH
IN_SIZES = (3 * DN_WIDTH, DN_WIDTH, 2 * DN_HEADS, 2 * DN_HEADS,
            RET_WIDTH, RET_WIDTH, RET_WIDTH, RET_WIDTH,
            ATT_WIDTH, ATT_WIDTH, ATT_WIDTH)
IN_WIDTH = 4 * DN_WIDTH + 4 * DN_HEADS + 4 * RET_WIDTH + 3 * ATT_WIDTH
CONV_W = 4
DN_CHUNK = 64
RET_CHUNK = 128
RET_THETA = 10000.0
ROPE_THETA = 500000.0
ROPE_DIM = HEAD_DIM // 4
DILATED_PATTERNS = ((128, 1), (512, 4), (2048, 16))
N_EXPERTS = 32
TOP_K = 4
D_FF = D_MODEL
SWIGLU_ALPHA = 1.702
SWIGLU_LIMIT = 7.0
MOE_BLOCK = 256
PLE_DIM = 256
NORM_EPS = 1e-6
NEG_BIG = -1e30

kernel_name = 'hybrid_bidir_parallel_heads_encoder'

F32 = jnp.float32


def rms_norm(x, g):
    xf = x.astype(F32)
    y = xf * lax.rsqrt(jnp.mean(xf * xf, axis=-1, keepdims=True) + NORM_EPS)
    return (y * g.astype(F32)).astype(x.dtype)


def l2_normalize(x):
    return x * lax.rsqrt(jnp.sum(x * x, axis=-1, keepdims=True) + NORM_EPS)


def to_heads(t, n_heads):
    b, s, w = t.shape
    return t.reshape(b, s, n_heads, w // n_heads).transpose(0, 2, 1, 3)


def from_heads(t):
    b, h, s, d = t.shape
    return t.transpose(0, 2, 1, 3).reshape(b, s, h * d)


def apply_rotary(x, inv_freq):
    half = inv_freq.shape[0]
    rot = 2 * half
    pos = jnp.arange(x.shape[2], dtype=F32)
    ang = pos[:, None] * inv_freq[None, :]
    cos, sin = jnp.cos(ang), jnp.sin(ang)
    x1, x2 = x[..., :half], x[..., half:rot]
    return jnp.concatenate([x1 * cos - x2 * sin, x1 * sin + x2 * cos, x[..., rot:]], axis=-1)


def depthwise_conv_centred(x, w):
    width = w.shape[0]
    pad_l = width // 2
    pad_r = width - 1 - pad_l
    return lax.conv_general_dilated(x, w[:, None, :], window_strides=(1,), padding=[(pad_l, pad_r)],
                                    dimension_numbers=('NWC', 'WIO', 'NWC'),
                                    feature_group_count=x.shape[-1])


def gated_delta_chunked(q, k, v, g, beta):
    B, H, S, dk = q.shape
    dv = v.shape[-1]
    C = DN_CHUNK
    N = S // C
    qc = q.reshape(B, H, N, C, dk)
    kc = k.reshape(B, H, N, C, dk)
    vc = v.reshape(B, H, N, C, dv)
    bc = beta.reshape(B, H, N, C, 1)
    gc = jnp.cumsum(g.reshape(B, H, N, C), axis=-1)
    lower = jnp.tril(jnp.ones((C, C), dtype=bool))
    strict = jnp.tril(jnp.ones((C, C), dtype=bool), -1)
    diff = gc[..., :, None] - gc[..., None, :]
    decay = jnp.where(lower, jnp.exp(jnp.where(lower, diff, 0.0)), 0.0)
    kb = kc * bc
    a_mat = jnp.where(strict, jnp.einsum('bhnid,bhnjd->bhnij', kb, kc) * decay, 0.0)
    eye = jnp.eye(C, dtype=q.dtype)
    rhs = jnp.concatenate([vc * bc, kb * jnp.exp(gc)[..., None]], axis=-1)
    sol = lax.linalg.triangular_solve(a_mat + eye, rhs, left_side=True, lower=True, unit_diagonal=True)
    w_v, w_k = sol[..., :dv], sol[..., dv:]
    attn = jnp.einsum('bhnid,bhnjd->bhnij', qc, kc) * decay
    q_dec = qc * jnp.exp(gc)[..., None]
    k_dec = kc * jnp.exp(gc[..., -1:] - gc)[..., None]
    g_last = jnp.exp(gc[..., -1])

    def step(state, xs):
        wv, wk, qd, at, kd, gl = xs
        v_new = wv - jnp.einsum('bhck,bhkv->bhcv', wk, state)
        o = jnp.einsum('bhck,bhkv->bhcv', qd, state) + jnp.einsum('bhij,bhjv->bhiv', at, v_new)
        state = state * gl[..., None, None] + jnp.einsum('bhck,bhcv->bhkv', kd, v_new)
        return state, o

    xs = tuple(jnp.moveaxis(t, 2, 0) for t in (w_v, w_k, q_dec, attn, k_dec, g_last))
    s0 = jnp.zeros((B, H, dk, dv), q.dtype)
    _, o = lax.scan(step, s0, xs)
    return jnp.moveaxis(o, 0, 2).reshape(B, H, S, dv)


def deltanet_mixer(qkv, z, a, b, conv_w, a_log, dt_bias, norm_w):
    B, S, _ = qkv.shape
    qkv = jax.nn.silu(depthwise_conv_centred(qkv.astype(F32), conv_w.astype(F32)))
    q, k, v = jnp.split(qkv, 3, axis=-1)
    q = l2_normalize(to_heads(q, DN_HEADS)) * (HEAD_DIM ** -0.5)
    k = l2_normalize(to_heads(k, DN_HEADS))
    v = to_heads(v, DN_HEADS)
    a = a.astype(F32).reshape(B, S, 2, DN_HEADS)
    b = b.astype(F32).reshape(B, S, 2, DN_HEADS)
    g = -jnp.exp(a_log.astype(F32)) * jax.nn.softplus(a + dt_bias.astype(F32))
    beta = jax.nn.sigmoid(b)
    g = g.transpose(2, 0, 3, 1)
    beta = beta.transpose(2, 0, 3, 1)
    o_f = gated_delta_chunked(q, k, v, g[0], beta[0])
    o_b = jnp.flip(gated_delta_chunked(jnp.flip(q, 2), jnp.flip(k, 2), jnp.flip(v, 2),
                                       jnp.flip(g[1], -1), jnp.flip(beta[1], -1)), 2)
    o = (o_f + o_b).transpose(0, 2, 1, 3)
    o = rms_norm(o, norm_w) * jax.nn.silu(z.astype(F32).reshape(B, S, DN_HEADS, HEAD_DIM))
    return o.reshape(B, S, DN_WIDTH)


def retention_chunked(q, k, v, gamma):
    B, H, S, dk = q.shape
    dv = v.shape[-1]
    C = RET_CHUNK
    N = S // C
    lg = jnp.log(gamma)[:, None]
    idx = jnp.arange(C, dtype=F32)
    rel = idx[:, None] - idx[None, :]
    low = rel >= 0
    dmat = jnp.where(low, jnp.exp(jnp.where(low, rel, 0.0)[None] * lg[:, :, None]), 0.0)
    qc = q.reshape(B, H, N, C, dk)
    kc = k.reshape(B, H, N, C, dk)
    vc = v.reshape(B, H, N, C, dv)
    scores = jnp.einsum('bhnid,bhnjd->bhnij', qc, kc) * dmat[:, None]
    inner = jnp.einsum('bhnij,bhnjv->bhniv', scores, vc)
    k_decay = jnp.exp((C - 1.0 - idx)[None, :] * lg)
    kv = jnp.einsum('bhnjd,bhnjv->bhndv', kc * k_decay[:, None, :, None], vc)
    chunk_decay = jnp.exp(C * lg)[:, :, None]

    def step(r, kv_n):
        return r * chunk_decay + kv_n, r

    _, r_prev = lax.scan(step, jnp.zeros((B, H, dk, dv), q.dtype), jnp.moveaxis(kv, 2, 0))
    r_prev = jnp.moveaxis(r_prev, 0, 2)
    q_decay = jnp.exp((idx + 1.0)[None, :] * lg)
    cross = jnp.einsum('bhnid,bhndv->bhniv', qc * q_decay[:, None, :, None], r_prev)
    return (inner + cross).reshape(B, H, S, dv)


def retention_mixer(q, k, v, gate, decay_logit, norm_w):
    B, S, _ = q.shape
    inv = 1.0 / jnp.power(jnp.float32(RET_THETA), jnp.linspace(0.0, 1.0, HEAD_DIM // 2, dtype=F32))
    q = apply_rotary(to_heads(q.astype(F32), RET_HEADS), inv)
    k = apply_rotary(to_heads(k.astype(F32), RET_HEADS), inv) * (HEAD_DIM ** -0.5)
    v = to_heads(v.astype(F32), RET_HEADS)
    gam = jax.nn.sigmoid(decay_logit.astype(F32))
    o = retention_chunked(q, k, v, gam[0]) + jnp.flip(
        retention_chunked(jnp.flip(q, 2), jnp.flip(k, 2), jnp.flip(v, 2), gam[1]), 2)
    o = o.transpose(0, 2, 1, 3)
    mu = jnp.mean(o, axis=-1, keepdims=True)
    oc = o - mu
    o = oc * lax.rsqrt(jnp.mean(oc * oc, axis=-1, keepdims=True) + NORM_EPS)
    o = o.reshape(B, S, RET_WIDTH) * norm_w.astype(F32)
    return jax.nn.silu(gate.astype(F32)) * o


def strided_window_attention(q, k, v, dil, half):
    B, H, S, d = q.shape
    L = S // dil
    QB = half
    nb = -(-L // QB)
    Lp = nb * QB

    def streams(t):
        return t.reshape(B, H, L, dil, d).swapaxes(2, 3)

    qs = jnp.pad(streams(q), ((0, 0), (0, 0), (0, 0), (0, Lp - L), (0, 0))).reshape(B, H, dil, nb, QB, d)
    kpad = ((0, 0), (0, 0), (0, 0), (QB, Lp - L + QB), (0, 0))
    kb = jnp.pad(streams(k), kpad).reshape(B, H, dil, nb + 2, QB, d)
    vb = jnp.pad(streams(v), kpad).reshape(B, H, dil, nb + 2, QB, d)
    kwin = jnp.concatenate([kb[:, :, :, :-2], kb[:, :, :, 1:-1], kb[:, :, :, 2:]], axis=4)
    vwin = jnp.concatenate([vb[:, :, :, :-2], vb[:, :, :, 1:-1], vb[:, :, :, 2:]], axis=4)
    s = jnp.einsum('bhrnqd,bhrnkd->bhrnqk', qs, kwin) * (d ** -0.5)
    qpos = jnp.arange(nb)[:, None] * QB + jnp.arange(QB)[None, :]
    kpos = jnp.arange(nb)[:, None] * QB - QB + jnp.arange(3 * QB)[None, :]
    mask = ((jnp.abs(qpos[:, :, None] - kpos[:, None, :]) <= half)
            & (kpos >= 0)[:, None, :] & (kpos < L)[:, None, :])
    s = jnp.where(mask, s, NEG_BIG)
    m = jnp.max(s, axis=-1)
    p = jnp.where(mask, jnp.exp(s - m[..., None]), 0.0)
    l = jnp.sum(p, axis=-1)
    acc = jnp.einsum('bhrnqk,bhrnkd->bhrnqd', p, vwin)

    def unstream(t):
        t = t.reshape(B, H, dil, Lp, *t.shape[5:])[:, :, :, :L]
        return jnp.swapaxes(t, 2, 3).reshape(B, H, S, *t.shape[4:])

    return unstream(m), unstream(l), unstream(acc)


def dilated_mixer(q, k, v):
    inv = jnp.power(jnp.float32(ROPE_THETA), -jnp.arange(0, ROPE_DIM, 2, dtype=F32) / ROPE_DIM)
    q = apply_rotary(to_heads(q.astype(F32), ATT_HEADS), inv)
    k = apply_rotary(to_heads(k.astype(F32), ATT_HEADS), inv)
    v = to_heads(v.astype(F32), ATT_HEADS)
    ms, ls, accs = [], [], []
    for window, dil in DILATED_PATTERNS:
        m, l, acc = strided_window_attention(q, k, v, dil, window // (2 * dil))
        ms.append(m)
        ls.append(l)
        accs.append(acc)
    m_all = jnp.stack(ms)
    l_all = jnp.stack(ls)
    acc_all = jnp.stack(accs)
    w = jnp.exp(m_all - jnp.max(m_all, axis=0))
    o = jnp.sum(w[..., None] * acc_all, axis=0) / jnp.sum(w * l_all, axis=0)[..., None]
    return from_heads(o)


def moe_ffn(h, router_w, router_b, w_gate, b_gate, w_up, b_up, w_down, b_down):
    B, S, D = h.shape
    T = B * S
    TK = T * TOP_K
    hf = h.reshape(T, D)
    logits = hf.astype(F32) @ router_w.astype(F32) + router_b.astype(F32)
    top_val, top_idx = lax.top_k(logits, TOP_K)
    gates = jax.nn.softmax(top_val, axis=-1)
    flat_e = top_idx.reshape(TK)
    flat_tok = jnp.repeat(jnp.arange(T, dtype=jnp.int32), TOP_K)
    order = jnp.argsort(flat_e)
    se, st, sg = flat_e[order], flat_tok[order], gates.reshape(TK)[order]
    counts = jnp.bincount(flat_e, length=N_EXPERTS)
    padded = (counts + MOE_BLOCK - 1) // MOE_BLOCK * MOE_BLOCK
    start = jnp.cumsum(counts) - counts
    pend = jnp.cumsum(padded)
    pstart = pend - padded
    dest = pstart[se] + jnp.arange(TK) - start[se]
    n_blocks = (TK + N_EXPERTS * (MOE_BLOCK - 1) + MOE_BLOCK - 1) // MOE_BLOCK
    n_rows = n_blocks * MOE_BLOCK
    row_tok = jnp.zeros((n_rows,), jnp.int32).at[dest].set(st)
    block_exp = jnp.minimum(jnp.searchsorted(pend, jnp.arange(n_blocks) * MOE_BLOCK, side='right'),
                            N_EXPERTS - 1)
    xin = hf[row_tok].reshape(n_blocks, MOE_BLOCK, D)

    def expert_block(args):
        xb, e = args
        gt = jnp.minimum(xb @ w_gate[e] + b_gate[e], SWIGLU_LIMIT)
        up = jnp.clip(xb @ w_up[e] + b_up[e], -SWIGLU_LIMIT, SWIGLU_LIMIT)
        hid = (up + 1.0) * gt * jax.nn.sigmoid(SWIGLU_ALPHA * gt)
        return hid @ w_down[e] + b_down[e]

    yrows = lax.map(expert_block, (xin, block_exp)).reshape(n_rows, D)
    out = jnp.zeros((T, D), F32).at[st].add(yrows[dest].astype(F32) * sg[:, None])
    return out.astype(h.dtype).reshape(B, S, D)


def _layer(x, p_i, norm1, w_in, conv_w, dn_a_log, dn_dt_bias, dn_norm, ret_decay, ret_norm,
           w_out, norm2, router_w, router_b, w_gate, b_gate, w_up, b_up, w_down, b_down,
           ple_proj, ple_gate, ple_norm):
    h = rms_norm(x, norm1)
    proj = jnp.einsum('bsd,de->bse', h, w_in)
    offs = np.cumsum(IN_SIZES)[:-1].tolist()
    (dn_qkv, dn_z, dn_a, dn_b, r_q, r_k, r_v, r_g, a_q, a_k, a_v) = jnp.split(proj, offs, axis=-1)
    o_dn = deltanet_mixer(dn_qkv, dn_z, dn_a, dn_b, conv_w, dn_a_log, dn_dt_bias, dn_norm)
    o_ret = retention_mixer(r_q, r_k, r_v, r_g, ret_decay, ret_norm)
    o_att = dilated_mixer(a_q, a_k, a_v)
    mix = jnp.concatenate([o_dn, o_ret, o_att], axis=-1).astype(x.dtype)
    x = x + jnp.einsum('bsm,md->bsd', mix, w_out)
    x = x + moe_ffn(rms_norm(x, norm2), router_w, router_b, w_gate, b_gate, w_up, b_up, w_down, b_down)
    gate = jax.nn.sigmoid(jnp.einsum('bsd,de->bse', x, ple_gate).astype(F32))
    e = jnp.einsum('bsp,pd->bsd', p_i, ple_proj).astype(F32) * gate
    return x + rms_norm(e, ple_norm).astype(x.dtype)


def _trunk(x, p, final_norm, layer_params):
    for i in range(DEPTH):
        x = _layer(x, p[i], *[w[i] for w in layer_params])
    return rms_norm(x, final_norm)


def setup_inputs(seed: int = 0) -> dict:
    key = jax.random.key(seed)
    ks = jax.random.split(key, 28)
    L, D, E, F = DEPTH, D_MODEL, N_EXPERTS, D_FF

    def nrm(k, shape, scale):
        return jax.random.normal(k, shape, F32) * scale

    def gain(k, shape):
        return 1.0 + 0.01 * jax.random.normal(k, shape, F32)

    gam0 = 1.0 - jnp.power(2.0, -5.0 - jnp.arange(RET_HEADS, dtype=F32))
    ret_decay = jnp.log(gam0 / (1.0 - gam0))[None, None, :] + 0.1 * jax.random.normal(ks[0], (L, 2, RET_HEADS), F32)
    dt = jnp.exp(jax.random.uniform(ks[1], (L, 2, DN_HEADS), F32, math.log(1e-3), math.log(1e-1)))
    dn_dt_bias = dt + jnp.log(-jnp.expm1(-dt))
    dn_a_log = jnp.log(jax.random.uniform(ks[2], (L, 2, DN_HEADS), F32, 1.0, 16.0))
    return {
        'x_prompt': jax.random.normal(ks[3], (BATCH, SEQ, D), F32),
        'x_sample': jax.random.normal(ks[4], (DEC_BATCH, DEC_SEQ, D), F32),
        'p_prompt': jax.random.normal(ks[5], (L, BATCH, SEQ, PLE_DIM), F32),
        'p_sample': jax.random.normal(ks[6], (L, DEC_BATCH, DEC_SEQ, PLE_DIM), F32),
        'norm1': gain(ks[7], (L, D)),
        'w_in': nrm(ks[8], (L, D, IN_WIDTH), D ** -0.5),
        'conv_w': nrm(ks[9], (L, CONV_W, 3 * DN_WIDTH), CONV_W ** -0.5),
        'dn_a_log': dn_a_log,
        'dn_dt_bias': dn_dt_bias,
        'dn_norm': gain(ks[10], (L, HEAD_DIM)),
        'ret_decay': ret_decay,
        'ret_norm': gain(ks[11], (L, RET_WIDTH)),
        'w_out': nrm(ks[12], (L, MIX_WIDTH, D), 0.5 * MIX_WIDTH ** -0.5),
        'norm2': gain(ks[13], (L, D)),
        'router_w': nrm(ks[14], (L, D, E), D ** -0.5),
        'router_b': nrm(ks[15], (L, E), 0.01),
        'w_gate': nrm(ks[16], (L, E, D, F), D ** -0.5),
        'b_gate': nrm(ks[17], (L, E, F), 0.01),
        'w_up': nrm(ks[18], (L, E, D, F), D ** -0.5),
        'b_up': nrm(ks[19], (L, E, F), 0.01),
        'w_down': nrm(ks[20], (L, E, F, D), 0.5 * F ** -0.5),
        'b_down': nrm(ks[21], (L, E, D), 0.01),
        'ple_proj': nrm(ks[22], (L, PLE_DIM, D), PLE_DIM ** -0.5),
        'ple_gate': nrm(ks[23], (L, D, D), D ** -0.5),
        'ple_norm': gain(ks[24], (L, D)),
        'final_norm': gain(ks[25], (D,)),
    }


def reference(x_prompt, x_sample, p_prompt, p_sample, norm1, w_in, conv_w, dn_a_log, dn_dt_bias,
              dn_norm, ret_decay, ret_norm, w_out, norm2, router_w, router_b, w_gate, b_gate,
              w_up, b_up, w_down, b_down, ple_proj, ple_gate, ple_norm, final_norm):
    layer_params = (norm1, w_in, conv_w, dn_a_log, dn_dt_bias, dn_norm, ret_decay, ret_norm,
                    w_out, norm2, router_w, router_b, w_gate, b_gate, w_up, b_up, w_down, b_down,
                    ple_proj, ple_gate, ple_norm)
    y_prompt = _trunk(x_prompt, p_prompt, final_norm, layer_params)
    y_sample = _trunk(x_sample, p_sample, final_norm, layer_params)
    return (y_prompt, y_sample)
```

```python
import functools
import math

import jax
import jax.numpy as jnp
import numpy as np
from jax import lax
from jax.experimental import pallas as pl
from jax.experimental.pallas import tpu as pltpu

F32 = jnp.float32
BF16 = jnp.bfloat16
I32 = jnp.int32

D_MODEL = 1024
HEAD_DIM = 64
DN_HEADS = 4
RET_HEADS = 4
ATT_HEADS = 8
DN_WIDTH = 256
RET_WIDTH = 256
ATT_WIDTH = 512
CONV_W = 4
RET_THETA = 10000.0
ROPE_THETA = 500000.0
ROPE_DIM = HEAD_DIM // 4
DILATED_PATTERNS = ((128, 1), (512, 4), (2048, 16))
ATT_HALF = 64
N_EXPERTS = 32
TOP_K = 4
SWIGLU_ALPHA = 1.702
SWIGLU_LIMIT = 7.0
MOE_BLOCK = 256
PLE_DIM = 256
NORM_EPS = 1e-6
NEG_BIG = -1e30

LANES = 128
SUBLANES = 8
VMEM_LIMIT = 56 * 1024 * 1024

COL_DN_QKV = 0
COL_DN_Z = 768
COL_RQ = 1024
COL_RK = 1280
COL_RV = 1536
COL_RG = 1792
COL_AQ = 2048
COL_AK = 2560
COL_AV = 3072
COL_AB = 3584
PROJ_W = 3840

CHUNK = 64
CAT = 4 * CHUNK


class Cfg:
    def __init__(self, b0, s0, b1, s1):
        self.b0, self.s0, self.b1, self.s1 = b0, s0, b1, s1
        self.t0 = b0 * s0
        self.t1 = b1 * s1
        self.t = self.t0 + self.t1
        self.smax = max(s0, s1)

    def pos(self, row):
        return jnp.where(row < self.t0, row % self.s0, (row - self.t0) % self.s1)

    def seq_len(self, row):
        return jnp.where(row < self.t0, self.s0, self.s1)


def _cparams(n_axes=1, sem=None):
    return pltpu.CompilerParams(
        dimension_semantics=tuple(sem or ("arbitrary",) * n_axes),
        vmem_limit_bytes=VMEM_LIMIT)


def _split_bf16(x):
    hi = x.astype(BF16)
    lo = (x - hi.astype(F32)).astype(BF16)
    return hi, lo


def _dot(a, b):
    return jnp.dot(a, b, preferred_element_type=F32)


def _dot_x2(a, b_exact):
    hi, lo = _split_bf16(a)
    return _dot(hi, b_exact) + _dot(lo, b_exact)


def _dot_lx2(a_exact, b):
    hi, lo = _split_bf16(b)
    return _dot(a_exact, hi) + _dot(a_exact, lo)


def _dot_x3(a, b):
    ah, al = _split_bf16(a)
    bh, bl = _split_bf16(b)
    return _dot(ah, bh) + (_dot(al, bh) + _dot(ah, bl))


def _sigmoid(x):
    return 1.0 / (1.0 + jnp.exp(-x))


def _silu(x):
    return x * _sigmoid(x)


def _head_sum_matrix(width):
    r = lax.broadcasted_iota(I32, (width, width), 0) // HEAD_DIM
    c = lax.broadcasted_iota(I32, (width, width), 1) // HEAD_DIM
    return jnp.where(r == c, 1.0, 0.0).astype(BF16)


_SECTIONS = tuple((c, 1 if c in (COL_RQ, COL_RK) else 2 if COL_AQ <= c < COL_AV else 0)
                  for c in range(0, PROJ_W, 256))


def _rope_tables(smax):
    pos = jnp.arange(smax, dtype=F32)[:, None]
    d = jnp.arange(256) % HEAD_DIM

    def build(inv, half):
        rot = 2 * half
        ang = pos * inv[d % half][None, :]
        cos, sin = jnp.cos(ang), jnp.sin(ang)
        in_rot = (d < rot)[None, :]
        first = (d < half)[None, :]
        c = jnp.where(in_rot, cos, 1.0)
        s_plus = jnp.where(in_rot & ~first, sin, 0.0)
        s_minus = jnp.where(first, -sin, 0.0)
        return jnp.stack([c, s_plus, s_minus])

    inv_ret = 1.0 / jnp.power(jnp.float32(RET_THETA), jnp.linspace(0.0, 1.0, HEAD_DIM // 2, dtype=F32))
    inv_att = jnp.power(jnp.float32(ROPE_THETA), -jnp.arange(0, ROPE_DIM, 2, dtype=F32) / ROPE_DIM)
    return build(inv_ret, HEAD_DIM // 2), build(inv_att, ROPE_DIM // 2)


def _in_proj_kernel(x_ref, g_ref, w_ref, tr_ref, ta_ref, o_ref):
    x = x_ref[...]
    y = x * lax.rsqrt(jnp.mean(x * x, axis=-1, keepdims=True) + NORM_EPS)
    h = (y * g_ref[...]).astype(BF16)
    for col, kind in _SECTIONS:
        acc = _dot(h, w_ref[:, col:col + 256])
        if kind:
            t_ref, half = (tr_ref, HEAD_DIM // 2) if kind == 1 else (ta_ref, ROPE_DIM // 2)
            acc = (acc * t_ref[0] + pltpu.roll(acc, half, 1) * t_ref[1]
                   + pltpu.roll(acc, 256 - half, 1) * t_ref[2])
        o_ref[:, col:col + 256] = acc


def _in_proj(cfg, x, g1, w_bf16, tab_ret, tab_att, tm):
    n = cfg.t // tm

    def tab_map(i):
        return (0, cfg.pos(i * tm) // tm, 0)

    return pl.pallas_call(
        _in_proj_kernel,
        grid=(n,),
        in_specs=[
            pl.BlockSpec((tm, D_MODEL), lambda i: (i, 0)),
            pl.BlockSpec((1, D_MODEL), lambda i: (0, 0)),
            pl.BlockSpec((D_MODEL, PROJ_W), lambda i: (0, 0)),
            pl.BlockSpec((3, tm, 256), tab_map),
            pl.BlockSpec((3, tm, 256), tab_map),
        ],
        out_specs=pl.BlockSpec((tm, PROJ_W), lambda i: (i, 0)),
        out_shape=jax.ShapeDtypeStruct((cfg.t, PROJ_W), F32),
        compiler_params=_cparams(),
        name="in_proj",
    )(x, g1.reshape(1, D_MODEL), w_bf16, tab_ret, tab_att)


def _reorder_w_in(w_in):
    ab = w_in[:, 1024:1040]
    rest = w_in[:, 1040:]
    pad = jnp.zeros((w_in.shape[0], PROJ_W - 3600), w_in.dtype)
    return jnp.concatenate([w_in[:, :1024], rest, ab, pad], axis=1).astype(BF16)


def _dn_prep_kernel(cfg, tm, cur_ref, prev_ref, next_ref, ab_ref, cw_ref, ga_ref, gb_ref,
                    qkv_ref, gate_ref, ext_ref):
    i = pl.program_id(0)
    pos = cfg.pos(i * tm)
    first = pos == 0
    last = pos + tm == cfg.seq_len(i * tm)
    ext_ref[0:8, :] = jnp.where(first, 0.0, prev_ref[...])
    ext_ref[8:8 + tm, :] = cur_ref[...]
    ext_ref[8 + tm:16 + tm, :] = jnp.where(last, 0.0, next_ref[...])
    y = cw_ref[0:1, :] * ext_ref[6:6 + tm, :]
    for j in range(1, CONV_W):
        y = y + cw_ref[j:j + 1, :] * ext_ref[6 + j:6 + j + tm, :]
    y = _silu(y)
    hsum = _head_sum_matrix(DN_WIDTH)
    q = y[:, 0:DN_WIDTH]
    k = y[:, DN_WIDTH:2 * DN_WIDTH]
    qkv_ref[:, 0:DN_WIDTH] = q * lax.rsqrt(_dot_x2(q * q, hsum) + NORM_EPS) * (HEAD_DIM ** -0.5)
    qkv_ref[:, DN_WIDTH:2 * DN_WIDTH] = k * lax.rsqrt(_dot_x2(k * k, hsum) + NORM_EPS)
    qkv_ref[:, 2 * DN_WIDTH:] = y[:, 2 * DN_WIDTH:]
    ab = ab_ref[:, 0:LANES]
    xa = ab + gb_ref[...]
    softplus = jnp.maximum(xa, 0.0) + jnp.log(1.0 + jnp.exp(-jnp.abs(xa)))
    lane = lax.broadcasted_iota(I32, ab.shape, 1)
    gate_ref[...] = jnp.where(lane < 2 * DN_HEADS, ga_ref[...] * softplus, _sigmoid(ab))


def _dn_prep(cfg, proj, conv_w, a_log, dt_bias, tm):
    n = cfg.t // tm
    w3 = 3 * DN_WIDTH
    nh = tm // 8
    last8 = cfg.t // 8 - 1
    neg_a = jnp.zeros((1, LANES), F32).at[0, :2 * DN_HEADS].set(-jnp.exp(a_log.reshape(-1)))
    dtb = jnp.zeros((1, LANES), F32).at[0, :2 * DN_HEADS].set(dt_bias.reshape(-1))
    return pl.pallas_call(
        functools.partial(_dn_prep_kernel, cfg, tm),
        grid=(n,),
        in_specs=[
            pl.BlockSpec((tm, w3), lambda i: (i, 0)),
            pl.BlockSpec((8, w3), lambda i: (jnp.maximum(i * nh - 1, 0), 0)),
            pl.BlockSpec((8, w3), lambda i: (jnp.minimum((i + 1) * nh, last8), 0)),
            pl.BlockSpec((tm, 256), lambda i: (i, COL_AB // 256)),
            pl.BlockSpec((CONV_W, w3), lambda i: (0, 0)),
            pl.BlockSpec((1, LANES), lambda i: (0, 0)),
            pl.BlockSpec((1, LANES), lambda i: (0, 0)),
        ],
        out_specs=[
            pl.BlockSpec((tm, w3), lambda i: (i, 0)),
            pl.BlockSpec((tm, LANES), lambda i: (i, 0)),
        ],
        out_shape=[jax.ShapeDtypeStruct((cfg.t, w3), F32),
                   jax.ShapeDtypeStruct((cfg.t, LANES), F32)],
        scratch_shapes=[pltpu.VMEM((tm + 16, w3), F32)],
        compiler_params=_cparams(),
        name="dn_prep",
    )(proj, proj, proj, proj, conv_w, neg_a, dtb)


def _bd_mask():
    r = lax.broadcasted_iota(I32, (CAT, CAT), 0) // CHUNK
    c = lax.broadcasted_iota(I32, (CAT, CAT), 1) // CHUNK
    return r == c


def _bd(x, bdmask):
    xb = x.astype(BF16)
    return jnp.where(bdmask, jnp.concatenate([xb, xb, xb, xb], axis=0), jnp.zeros((), BF16))


def _dot_t0(a, b):
    return lax.dot_general(a, b, (((0,), (0,)), ((), ())), preferred_element_type=F32)


def _dot_t1(a, b):
    return lax.dot_general(a, b, (((1,), (1,)), ((), ())), preferred_element_type=F32)


def _chunk_ij(rows):
    i = lax.broadcasted_iota(I32, (rows, CAT), 0) % CHUNK
    j = lax.broadcasted_iota(I32, (rows, CAT), 1) % CHUNK
    return i, j


def _dn_stream(rev, qkv_ref, gate_ref, o_ref, s_ref, tm):
    nc = tm // CHUNK
    bdmask = _bd_mask()
    i, j = _chunk_ij(tm)
    l = lax.broadcasted_iota(I32, (LANES, 2 * CAT), 0)
    c = lax.broadcasted_iota(I32, (LANES, 2 * CAT), 1)
    base = (4 if rev else 0) + jnp.where(c >= CAT, 2 * DN_HEADS, 0)
    sel = jnp.where(l == base + (c % CAT) // CHUNK, 1.0, 0.0).astype(BF16)
    gb = _dot_x2(gate_ref[...], sel)
    g_cat = gb[:, :CAT]
    beta = gb[:, CAT:]
    ti = lax.broadcasted_iota(I32, (tm, tm), 0)
    tk = lax.broadcasted_iota(I32, (tm, tm), 1)
    same = (ti // CHUNK) == (tk // CHUNK)
    tri = jnp.where(same & ((tk >= ti) if rev else (tk <= ti)), 1.0, 0.0).astype(BF16)
    gc = _dot_lx2(tri, g_cat)
    between = (i < j) if rev else (i > j)
    diff = _dot_lx2(tri, jnp.where(between, g_cat, 0.0))
    incl = (i <= j) if rev else (i >= j)
    decay = jnp.where(incl, jnp.exp(diff), 0.0)
    egc = jnp.exp(gc)
    ic, jc = _chunk_ij(CHUNK)
    strict = (ic < jc) if rev else (ic > jc)
    eye = jnp.where(ic == jc, 1.0, 0.0)

    q_all = qkv_ref[:, 0:CAT]
    k_all = qkv_ref[:, CAT:2 * CAT]
    v_all = qkv_ref[:, 2 * CAT:3 * CAT]

    t_cats, attns = [], []
    for cidx in range(nc):
        r = slice(cidx * CHUNK, (cidx + 1) * CHUNK)
        kst = _bd(k_all[r], bdmask)
        qk_kk = _dot_t1(jnp.concatenate([q_all[r], k_all[r]], axis=0).astype(BF16), kst)
        attns.append(qk_kk[:CHUNK] * decay[r])
        a = jnp.where(strict, qk_kk[CHUNK:] * beta[r] * decay[r], 0.0)
        t = eye - jnp.where((ic >> 1) == (jc >> 1), a, 0.0)
        for lvl in range(2, 7):
            o = jnp.where(((ic >> lvl) == (jc >> lvl)) & ((ic >> (lvl - 1)) != (jc >> (lvl - 1))), a, 0.0)
            x = _dot(t.astype(BF16), _bd(o, bdmask))
            t = t - _dot(x.astype(BF16), _bd(t, bdmask))
        t_cats.append(t)

    order = range(nc - 1, -1, -1) if rev else range(nc)
    s = s_ref[...]
    for cidx in order:
        r = slice(cidx * CHUNK, (cidx + 1) * CHUNK)
        kc, qc, vc, bc, ec, gcc = k_all[r], q_all[r], v_all[r], beta[r], egc[r], gc[r]
        x1 = _dot(jnp.concatenate([kc * bc * ec, qc * ec], axis=0).astype(BF16), s.astype(BF16))
        u = vc * bc - x1[:CHUNK]
        v_new = _dot(t_cats[cidx].astype(BF16), _bd(u, bdmask))
        o_ref[r, :] = x1[CHUNK:] + _dot(attns[cidx].astype(BF16), _bd(v_new, bdmask))
        g_last = gcc[0:1] if rev else gcc[CHUNK - 1:CHUNK]
        k_dec = kc * jnp.exp(g_last - gcc)
        s = s * jnp.exp(g_last) + jnp.where(bdmask, _dot_t0(k_dec.astype(BF16), v_new.astype(BF16)), 0.0)
    s_ref[...] = s


def _dn_scan_kernel(cfg, tm, n, qf_ref, gf_ref, qb_ref, gb_ref, of_ref, ob_ref, sf_ref, sb_ref):
    i = pl.program_id(0)
    row_f = i * tm
    row_b = (n - 1 - i) * tm

    @pl.when(cfg.pos(row_f) == 0)
    def _():
        sf_ref[...] = jnp.zeros_like(sf_ref)

    @pl.when(cfg.pos(row_b) + tm == cfg.seq_len(row_b))
    def _():
        sb_ref[...] = jnp.zeros_like(sb_ref)

    _dn_stream(False, qf_ref, gf_ref, of_ref, sf_ref, tm)
    _dn_stream(True, qb_ref, gb_ref, ob_ref, sb_ref, tm)


def _dn_scan(cfg, qkv, gate, tm):
    n = cfg.t // tm
    w3 = 3 * DN_WIDTH
    fwd = lambda i: (i, 0)
    bwd = lambda i: (n - 1 - i, 0)
    return pl.pallas_call(
        functools.partial(_dn_scan_kernel, cfg, tm, n),
        grid=(n,),
        in_specs=[
            pl.BlockSpec((tm, w3), fwd), pl.BlockSpec((tm, LANES), fwd),
            pl.BlockSpec((tm, w3), bwd), pl.BlockSpec((tm, LANES), bwd),
        ],
        out_specs=[pl.BlockSpec((tm, DN_WIDTH), fwd), pl.BlockSpec((tm, DN_WIDTH), bwd)],
        out_shape=[jax.ShapeDtypeStruct((cfg.t, DN_WIDTH), F32)] * 2,
        scratch_shapes=[pltpu.VMEM((CAT, CAT), F32)] * 2,
        compiler_params=_cparams(),
        name="dn_scan",
    )(qkv, gate, qkv, gate)


def _ret_tables(decay_logit):
    lg = jnp.log(jax.nn.sigmoid(decay_logit.astype(F32)))
    i = jnp.arange(CHUNK, dtype=F32)[:, None]
    j = (jnp.arange(CAT) % CHUNK).astype(F32)[None, :]
    scale = HEAD_DIM ** -0.5
    out = []
    for d in range(2):
        lgl = jnp.repeat(lg[d], CHUNK)[None, :]
        rel = (j - i) if d else (i - j)
        dmat = jnp.where(rel >= 0, jnp.exp(jnp.where(rel >= 0, rel, 0.0) * lgl), 0.0) * scale
        qdec = jnp.exp(((CHUNK - i) if d else (i + 1.0)) * lgl)
        kdec = jnp.exp((i if d else (CHUNK - 1.0 - i)) * lgl) * scale
        cd = jnp.broadcast_to(jnp.exp(CHUNK * lgl), (CHUNK, CAT))
        out.append(jnp.stack([dmat, qdec, kdec, cd]))
    return jnp.stack(out)


def _ret_stream(rev, q_ref, k_ref, v_ref, tab_ref, o_ref, s_ref, tm):
    nc = tm // CHUNK
    bdmask = _bd_mask()
    d = 1 if rev else 0
    dmat, qdec, kdec, cd = tab_ref[d, 0], tab_ref[d, 1], tab_ref[d, 2], tab_ref[d, 3]
    order = range(nc - 1, -1, -1) if rev else range(nc)
    s = s_ref[...]
    for cidx in order:
        r = slice(cidx * CHUNK, (cidx + 1) * CHUNK)
        qc, kc, vc = q_ref[r, :], k_ref[r, :], v_ref[r, :]
        scores = _dot_t1(qc.astype(BF16), _bd(kc, bdmask)) * dmat
        inner = _dot(scores.astype(BF16), _bd(vc, bdmask))
        cross = _dot((qc * qdec).astype(BF16), s.astype(BF16))
        o_ref[r, :] = inner + cross
        kv = _dot_t0((kc * kdec).astype(BF16), vc.astype(BF16))
        s = s * cd[0:1] + jnp.where(bdmask, kv, 0.0)
    s_ref[...] = s


def _ret_scan_kernel(cfg, tm, n, qf, kf, vf, qb, kb, vb, tab_ref, of_ref, ob_ref, sf_ref, sb_ref):
    i = pl.program_id(0)
    row_f = i * tm
    row_b = (n - 1 - i) * tm

    @pl.when(cfg.pos(row_f) == 0)
    def _():
        sf_ref[...] = jnp.zeros_like(sf_ref)

    @pl.when(cfg.pos(row_b) + tm == cfg.seq_len(row_b))
    def _():
        sb_ref[...] = jnp.zeros_like(sb_ref)

    _ret_stream(False, qf, kf, vf, tab_ref, of_ref, sf_ref, tm)
    _ret_stream(True, qb, kb, vb, tab_ref, ob_ref, sb_ref, tm)


def _ret_scan(cfg, proj, tables, tm):
    n = cfg.t // tm

    def spec(col, rev):
        cb = col // 256
        return pl.BlockSpec((tm, 256), (lambda i: (n - 1 - i, cb)) if rev else (lambda i: (i, cb)))

    return pl.pallas_call(
        functools.partial(_ret_scan_kernel, cfg, tm, n),
        grid=(n,),
        in_specs=[spec(COL_RQ, False), spec(COL_RK, False), spec(COL_RV, False),
                  spec(COL_RQ, True), spec(COL_RK, True), spec(COL_RV, True),
                  pl.BlockSpec((2, 4, CHUNK, CAT), lambda i: (0, 0, 0, 0))],
        out_specs=[pl.BlockSpec((tm, RET_WIDTH), lambda i: (i, 0)),
                   pl.BlockSpec((tm, RET_WIDTH), lambda i: (n - 1 - i, 0))],
        out_shape=[jax.ShapeDtypeStruct((cfg.t, RET_WIDTH), F32)] * 2,
        scratch_shapes=[pltpu.VMEM((CAT, CAT), F32)] * 2,
        compiler_params=_cparams(),
        name="ret_scan",
    )(proj, proj, proj, proj, proj, proj, tables)


ATT_TQ = 2048
ATT_HALO = 1024
ATT_QB = 128


def _ds(start, size, stride):
    return pl.ds(start, size) if stride == 1 else pl.ds(start, size, stride=stride)


def _att_kernel(cfg, q_ref, kp_ref, kc_ref, kn_ref, vp_ref, vc_ref, vn_ref, o_ref,
                kbuf, vbuf, m_sc, l_sc, acc_sc):
    tq, halo, qb = ATT_TQ, ATT_HALO, ATT_QB
    kw = qb + 2 * ATT_HALF
    i = pl.program_id(1)
    pos0 = cfg.pos(i * tq)
    slen = cfg.seq_len(i * tq)
    kbuf[0:halo, :] = kp_ref[...]
    kbuf[halo:halo + tq, :] = kc_ref[...]
    kbuf[halo + tq:, :] = kn_ref[...]
    vbuf[0:halo, :] = vp_ref[...]
    vbuf[halo:halo + tq, :] = vc_ref[...]
    vbuf[halo + tq:, :] = vn_ref[...]
    head0 = lax.broadcasted_iota(I32, (qb, LANES), 1) < HEAD_DIM
    qi = lax.broadcasted_iota(I32, (qb, kw), 0)
    kj = lax.broadcasted_iota(I32, (qb, kw), 1)
    band = (kj - qi >= 0) & (kj - qi <= 2 * ATT_HALF)
    scale = HEAD_DIM ** -0.5

    for p, (_, dil) in enumerate(DILATED_PATTERNS):
        def body(u, carry, p=p, dil=dil):
            r = u % dil
            b = u // dil
            qs = r + dil * qb * b
            ks = halo + r + dil * (qb * b - ATT_HALF)
            q = q_ref[_ds(qs, qb, dil), :]
            k = kbuf[_ds(ks, kw, dil), :].astype(BF16)
            v = vbuf[_ds(ks, kw, dil), :].astype(BF16)
            kpos = pos0 // dil + qb * b - ATT_HALF + kj
            mask = band & (kpos >= 0) & (kpos < slen // dil)
            ms, ls, accs = [], [], []
            for h in range(2):
                qh = jnp.where(head0 if h == 0 else ~head0, q, 0.0).astype(BF16)
                s = jnp.where(mask, _dot_t1(qh, k) * scale, NEG_BIG)
                m = jnp.max(s, axis=-1, keepdims=True)
                pe = jnp.where(mask, jnp.exp(s - m), 0.0)
                ms.append(m)
                ls.append(jnp.sum(pe, axis=-1, keepdims=True))
                accs.append(_dot(pe.astype(BF16), v))
            m_sc[p, _ds(qs, qb, dil), :] = jnp.where(head0, ms[0], ms[1])
            l_sc[p, _ds(qs, qb, dil), :] = jnp.where(head0, ls[0], ls[1])
            acc_sc[p, _ds(qs, qb, dil), :] = jnp.where(head0, accs[0], accs[1])
            return carry

        lax.fori_loop(0, tq // qb, body, 0)

    m_all = [m_sc[p] for p in range(3)]
    mx = jnp.maximum(jnp.maximum(m_all[0], m_all[1]), m_all[2])
    num = jnp.zeros((tq, LANES), F32)
    den = jnp.zeros((tq, LANES), F32)
    for p in range(3):
        w = jnp.exp(m_all[p] - mx)
        num = num + w * acc_sc[p]
        den = den + w * l_sc[p]
    o_ref[...] = num / den


def _attention(cfg, proj):
    tq, halo = ATT_TQ, ATT_HALO
    n = cfg.t // tq
    per = tq // halo
    nh = cfg.t // halo

    def cur(col):
        return pl.BlockSpec((tq, LANES), lambda hp, i: (i, col // LANES + hp))

    def prev(col):
        return pl.BlockSpec((halo, LANES), lambda hp, i: (jnp.maximum(i * per - 1, 0), col // LANES + hp))

    def nxt(col):
        return pl.BlockSpec((halo, LANES), lambda hp, i: (jnp.minimum((i + 1) * per, nh - 1), col // LANES + hp))

    return pl.pallas_call(
        functools.partial(_att_kernel, cfg),
        grid=(ATT_HEADS // 2, n),
        in_specs=[cur(COL_AQ), prev(COL_AK), cur(COL_AK), nxt(COL_AK),
                  prev(COL_AV), cur(COL_AV), nxt(COL_AV)],
        out_specs=pl.BlockSpec((tq, LANES), lambda hp, i: (i, hp)),
        out_shape=jax.ShapeDtypeStruct((cfg.t, ATT_WIDTH), F32),
        scratch_shapes=[pltpu.VMEM((tq + 2 * halo, LANES), F32)] * 2
                       + [pltpu.VMEM((3, tq, LANES), F32)] * 3,
        compiler_params=_cparams(2),
        name="attention",
    )(proj, proj, proj, proj, proj, proj, proj)


MOE_TM = 512
MOE_R = 4 * MOE_TM + 256
MOE_NG = MOE_R // SUBLANES
RT_ID, RT_RANK, RT_GATE = 0, 4, 8


def _rms(x, g):
    return x * lax.rsqrt(jnp.mean(x * x, axis=-1, keepdims=True) + NORM_EPS) * g


def _post_kernel(dnf, dnb, z_ref, rtf, rtb, rg_ref, att_ref, x_ref, wout_ref, dnn_ref, rtn_ref,
                 n2_ref, rw_ref, rb_ref, ltri_ref, x1_ref, h2_ref, rt_ref, cnt_ref):
    hsum = _head_sum_matrix(DN_WIDTH)
    inv = 1.0 / HEAD_DIM
    o = dnf[...] + dnb[...]
    dn = o * lax.rsqrt(_dot_x2(o * o, hsum) * inv + NORM_EPS) * dnn_ref[...] * _silu(z_ref[...])
    o = rtf[...] + rtb[...]
    oc = o - _dot_x2(o, hsum) * inv
    ret = oc * lax.rsqrt(_dot_x2(oc * oc, hsum) * inv + NORM_EPS) * rtn_ref[...] * _silu(rg_ref[...])
    mixed = (_dot(dn.astype(BF16), wout_ref[0:256, :]) + _dot(ret.astype(BF16), wout_ref[256:512, :])
             + _dot(att_ref[...].astype(BF16), wout_ref[512:1024, :]))
    x1 = x_ref[...] + mixed
    x1_ref[...] = x1
    h2 = _rms(x1, n2_ref[...])
    h2_ref[...] = h2.astype(BF16)
    hh, hl = _split_bf16(h2)
    logits = _dot(hh, rw_ref[0]) + (_dot(hl, rw_ref[0]) + _dot(hh, rw_ref[1])) + rb_ref[...]
    tm = logits.shape[0]
    lane = lax.broadcasted_iota(I32, (tm, LANES), 1)
    lane_f = lane.astype(F32)
    work = logits
    vals, ids = [], []
    for _ in range(TOP_K):
        m = jnp.max(work, axis=-1, keepdims=True)
        idx = jnp.min(jnp.where(work == m, lane_f, float(LANES)), axis=-1, keepdims=True).astype(I32)
        vals.append(m)
        ids.append(idx)
        work = jnp.where(lane == idx, -3e38, work)
    es = [jnp.exp(v - vals[0]) for v in vals]
    tot = es[0] + es[1] + es[2] + es[3]
    onehots = [jnp.where(lane == idx, 1.0, 0.0) for idx in ids]
    msum = onehots[0] + onehots[1] + onehots[2] + onehots[3]
    before = _dot(ltri_ref[...], msum.astype(BF16))
    rec = jnp.zeros((tm, LANES), I32)
    for k in range(TOP_K):
        rank = jnp.sum(onehots[k] * before, axis=-1, keepdims=True).astype(I32)
        gate_bits = lax.bitcast_convert_type(es[k] / tot, I32)
        rec = jnp.where(lane == RT_ID + k, ids[k], rec)
        rec = jnp.where(lane == RT_RANK + k, rank, rec)
        rec = jnp.where(lane == RT_GATE + k, gate_bits, rec)
    rt_ref[...] = rec
    cnt_ref[0] = (before[tm - 1:tm, :] + msum[tm - 1:tm, :]).astype(I32)


def _post(cfg, dn_f, dn_b, ret_f, ret_b, att, proj, x, w_out, dn_norm, ret_norm, norm2,
          router_w, router_b):
    tm = MOE_TM
    n = cfg.t // tm
    rw = jnp.zeros((D_MODEL, LANES), F32).at[:, :N_EXPERTS].set(router_w)
    rw_hi = rw.astype(BF16)
    rw2 = jnp.stack([rw_hi, (rw - rw_hi.astype(F32)).astype(BF16)])
    rb = jnp.full((1, LANES), NEG_BIG, F32).at[0, :N_EXPERTS].set(router_b)
    ltri = (jnp.arange(tm)[:, None] > jnp.arange(tm)[None, :]).astype(BF16)
    row = lambda w: pl.BlockSpec((tm, w), lambda i: (i, 0))
    col = lambda c: pl.BlockSpec((tm, 256), lambda i: (i, c // 256))
    const = lambda *s: pl.BlockSpec(s, lambda i: (0,) * len(s))
    return pl.pallas_call(
        _post_kernel,
        grid=(n,),
        in_specs=[row(256), row(256), col(COL_DN_Z), row(256), row(256), col(COL_RG), row(512),
                  row(D_MODEL), const(D_MODEL, D_MODEL), const(1, 256), const(1, 256),
                  const(1, D_MODEL), const(2, D_MODEL, LANES), const(1, LANES), const(tm, tm)],
        out_specs=[row(D_MODEL), row(D_MODEL), row(LANES), pl.BlockSpec((1, 1, LANES), lambda i: (i, 0, 0))],
        out_shape=[jax.ShapeDtypeStruct((cfg.t, D_MODEL), F32),
                   jax.ShapeDtypeStruct((cfg.t, D_MODEL), BF16),
                   jax.ShapeDtypeStruct((cfg.t, LANES), I32),
                   jax.ShapeDtypeStruct((n, 1, LANES), I32)],
        compiler_params=_cparams(),
        name="post_route",
    )(dn_f, dn_b, proj, ret_f, ret_b, proj, att, x, w_out.astype(BF16),
      jnp.tile(dn_norm, DN_HEADS).reshape(1, 256), ret_norm.reshape(1, 256),
      norm2.reshape(1, D_MODEL), rw2, rb, ltri)


def _route_tables(cnt, n_rows):
    n = cnt.shape[0]
    c8 = (cnt[:, :N_EXPERTS] + 7) // 8 * 8
    lstart = jnp.cumsum(c8, axis=1) - c8
    tot = jnp.sum(c8, axis=0)
    region = (tot + MOE_BLOCK - 1) // MOE_BLOCK * MOE_BLOCK
    pend = jnp.cumsum(region)
    gstart = (pend - region)[None, :] + jnp.cumsum(c8, axis=0) - c8
    lrow = (jnp.arange(MOE_NG) * SUBLANES)[None, :, None]
    seg_end = (lstart + c8)[:, None, :]
    e = jnp.sum((lrow >= seg_end).astype(I32), axis=2)
    ec = jnp.minimum(e, N_EXPERTS - 1)
    g = jnp.take_along_axis(gstart, ec, axis=1) + lrow[:, :, 0] - jnp.take_along_axis(lstart, ec, axis=1)
    gdest = jnp.where(e < N_EXPERTS, g, -1).reshape(-1).astype(I32)
    n_blocks = n_rows // MOE_BLOCK
    nvalid = (pend[-1] // MOE_BLOCK).astype(I32)
    bstart = jnp.minimum(jnp.arange(n_blocks), nvalid - 1) * MOE_BLOCK
    block_exp = jnp.minimum(jnp.searchsorted(pend, bstart, side='right'), N_EXPERTS - 1).astype(I32)
    lstart_p = jnp.zeros((n, 1, LANES), I32).at[:, 0, :N_EXPERTS].set(lstart.astype(I32))
    return lstart_p, gdest, block_exp, nvalid.reshape(1)


def _moe_rows(cfg):
    n_tiles = cfg.t // MOE_TM
    worst = TOP_K * cfg.t + n_tiles * N_EXPERTS * 7 + N_EXPERTS * (MOE_BLOCK - 1)
    return -(-worst // MOE_BLOCK) * MOE_BLOCK


def _local_dest(rt, lstart_row):
    lane = lax.broadcasted_iota(I32, rt.shape, 1)
    ls = lstart_row.astype(F32)
    out = []
    for k in range(TOP_K):
        eid = rt[:, RT_ID + k:RT_ID + k + 1]
        base = jnp.sum(jnp.where(lane == eid, ls, 0.0), axis=-1, keepdims=True).astype(I32)
        out.append(base + rt[:, RT_RANK + k:RT_RANK + k + 1])
    return out


def _group_copy(src, dst, sem, s, d):
    return pltpu.make_async_copy(src.at[pl.ds(s, SUBLANES)], dst.at[pl.ds(d, SUBLANES)], sem)


def _dispatch_kernel(gd_ref, h2_ref, rt_ref, ls_ref, xin_ref, xs_ref, sem):
    i = pl.program_id(0)
    ld = _local_dest(rt_ref[...], ls_ref[0])
    slot = lax.broadcasted_iota(I32, (MOE_TM, MOE_R), 1)
    hit = (slot == ld[0]) | (slot == ld[1]) | (slot == ld[2]) | (slot == ld[3])
    pt = jnp.where(hit, 1.0, 0.0).astype(BF16)
    xs_ref[...] = _dot_t0(pt, h2_ref[...])

    def start(j, c):
        d = gd_ref[i * MOE_NG + j]

        @pl.when(d >= 0)
        def _():
            _group_copy(xs_ref, xin_ref, sem, pl.multiple_of(j * SUBLANES, SUBLANES),
                        pl.multiple_of(d, SUBLANES)).start()
        return c

    def wait(j, c):
        d = gd_ref[i * MOE_NG + j]

        @pl.when(d >= 0)
        def _():
            _group_copy(xs_ref, xin_ref, sem, 0, 0).wait()
        return c

    lax.fori_loop(0, MOE_NG, start, 0)
    lax.fori_loop(0, MOE_NG, wait, 0)


def _dispatch(cfg, h2, rt, lstart, gdest, xin_init):
    tm = MOE_TM
    n = cfg.t // tm
    grid_spec = pltpu.PrefetchScalarGridSpec(
        num_scalar_prefetch=1,
        grid=(n,),
        in_specs=[pl.BlockSpec((tm, D_MODEL), lambda i, gd: (i, 0)),
                  pl.BlockSpec((tm, LANES), lambda i, gd: (i, 0)),
                  pl.BlockSpec((1, 1, LANES), lambda i, gd: (i, 0, 0)),
                  pl.BlockSpec(memory_space=pl.ANY)],
        out_specs=pl.BlockSpec(memory_space=pl.ANY),
        scratch_shapes=[pltpu.VMEM((MOE_R, D_MODEL), F32), pltpu.SemaphoreType.DMA(())],
    )

    def body(gd_ref, h2_ref, rt_ref, ls_ref, init_ref, xin_ref, xs_ref, sem):
        del init_ref
        _dispatch_kernel(gd_ref, h2_ref, rt_ref, ls_ref, xin_ref, xs_ref, sem)

    return pl.pallas_call(
        body,
        grid_spec=grid_spec,
        out_shape=jax.ShapeDtypeStruct(xin_init.shape, F32),
        input_output_aliases={4: 0},
        compiler_params=_cparams(),
        name="moe_dispatch",
    )(gdest, h2, rt, lstart, xin_init)


def _expert_kernel(layer, be_ref, nv_ref, x_ref, wg_ref, bg_ref, wu_ref, bu_ref, wd_ref, bd_ref,
                   y_ref, wgc, wuc, wdc):
    del layer
    b = pl.program_id(0)
    prev = be_ref[jnp.maximum(b - 1, 0)]

    @pl.when((b == 0) | (be_ref[b] != prev))
    def _():
        wgc[...] = wg_ref[...].astype(BF16)
        wuc[...] = wu_ref[...].astype(BF16)
        wdc[...] = wd_ref[...].astype(BF16)

    @pl.when(b < nv_ref[0])
    def _():
        xb = x_ref[...].astype(BF16)
        gt = jnp.minimum(_dot(xb, wgc[...]) + bg_ref[...], SWIGLU_LIMIT)
        up = jnp.clip(_dot(xb, wuc[...]) + bu_ref[...], -SWIGLU_LIMIT, SWIGLU_LIMIT)
        hid = (up + 1.0) * gt * _sigmoid(SWIGLU_ALPHA * gt)
        y_ref[...] = _dot(hid.astype(BF16), wdc[...]) + bd_ref[...]

    @pl.when(b >= nv_ref[0])
    def _():
        y_ref[...] = jnp.zeros_like(y_ref)


def _experts(layer, xin, block_exp, nvalid, w_gate, b_gate, w_up, b_up, w_down, b_down):
    n_blocks = xin.shape[0] // MOE_BLOCK
    d = D_MODEL
    nl, ne = w_gate.shape[0], w_gate.shape[1]
    xmap = lambda b, be, nv: (jnp.minimum(b, nv[0] - 1), 0)
    wmap = lambda b, be, nv: (layer, be[b], 0, 0)
    wspec = pl.BlockSpec((None, None, d, d), wmap)
    bspec = pl.BlockSpec((None, None, 1, d), wmap)
    grid_spec = pltpu.PrefetchScalarGridSpec(
        num_scalar_prefetch=2,
        grid=(n_blocks,),
        in_specs=[pl.BlockSpec((MOE_BLOCK, d), xmap), wspec, bspec, wspec, bspec, wspec, bspec],
        out_specs=pl.BlockSpec((MOE_BLOCK, d), lambda b, be, nv: (b, 0)),
        scratch_shapes=[pltpu.VMEM((d, d), BF16)] * 3,
    )
    return pl.pallas_call(
        functools.partial(_expert_kernel, layer),
        grid_spec=grid_spec,
        out_shape=jax.ShapeDtypeStruct(xin.shape, F32),
        compiler_params=_cparams(),
        name="moe_experts",
    )(block_exp, nvalid, xin, w_gate, b_gate.reshape(nl, ne, 1, d), w_up, b_up.reshape(nl, ne, 1, d),
      w_down, b_down.reshape(nl, ne, 1, d))


def _combine_kernel(final, gd_ref, rt_ref, ls_ref, x1_ref, p_ref, pg_ref, pp_ref, pn_ref, fn_ref,
                    y_ref, o_ref, ys_ref, sem):
    i = pl.program_id(0)

    @pl.when(i == 0)
    def _():
        ys_ref[...] = jnp.zeros_like(ys_ref)

    def start(j, c):
        d = gd_ref[i * MOE_NG + j]

        @pl.when(d >= 0)
        def _():
            _group_copy(y_ref, ys_ref, sem, pl.multiple_of(d, SUBLANES),
                        pl.multiple_of(j * SUBLANES, SUBLANES)).start()
        return c

    def wait(j, c):
        d = gd_ref[i * MOE_NG + j]

        @pl.when(d >= 0)
        def _():
            _group_copy(y_ref, ys_ref, sem, 0, 0).wait()
        return c

    lax.fori_loop(0, MOE_NG, start, 0)
    rt = rt_ref[...]
    ld = _local_dest(rt, ls_ref[0])
    slot = lax.broadcasted_iota(I32, (MOE_TM, MOE_R), 1)
    ptg = jnp.zeros((MOE_TM, MOE_R), F32)
    for k in range(TOP_K):
        gate = lax.bitcast_convert_type(rt[:, RT_GATE + k:RT_GATE + k + 1], F32)
        ptg = jnp.where(slot == ld[k], gate, ptg)
    lax.fori_loop(0, MOE_NG, wait, 0)
    x2 = x1_ref[...] + _dot(ptg.astype(BF16), ys_ref[...].astype(BF16))
    gate = _sigmoid(_dot(x2.astype(BF16), pg_ref[...]))
    e = _dot(p_ref[...].astype(BF16), pp_ref[...]) * gate
    x3 = x2 + _rms(e, pn_ref[...])
    if final:
        x3 = _rms(x3, fn_ref[...])
    o_ref[...] = x3


def _combine(cfg, final, rt, lstart, gdest, x1, p, ple_gate, ple_proj, ple_norm, final_norm, y):
    tm = MOE_TM
    n = cfg.t // tm
    row = lambda w: pl.BlockSpec((tm, w), lambda i, gd: (i, 0))
    const = lambda *s: pl.BlockSpec(s, lambda i, gd: (0,) * len(s))
    grid_spec = pltpu.PrefetchScalarGridSpec(
        num_scalar_prefetch=1,
        grid=(n,),
        in_specs=[row(LANES), pl.BlockSpec((1, 1, LANES), lambda i, gd: (i, 0, 0)), row(D_MODEL),
                  row(PLE_DIM), const(D_MODEL, D_MODEL), const(PLE_DIM, D_MODEL),
                  const(1, D_MODEL), const(1, D_MODEL), pl.BlockSpec(memory_space=pl.ANY)],
        out_specs=row(D_MODEL),
        scratch_shapes=[pltpu.VMEM((MOE_R, D_MODEL), F32), pltpu.SemaphoreType.DMA(())],
    )
    return pl.pallas_call(
        functools.partial(_combine_kernel, final),
        grid_spec=grid_spec,
        out_shape=jax.ShapeDtypeStruct((cfg.t, D_MODEL), F32),
        compiler_params=_cparams(),
        name="moe_combine",
    )(gdest, rt, lstart, x1, p, ple_gate.astype(BF16), ple_proj.astype(BF16),
      ple_norm.reshape(1, D_MODEL), final_norm.reshape(1, D_MODEL), y)


def _layer(cfg, layer, final, x, p_l, tabs, norm1, w_in, conv_w, dn_a_log, dn_dt_bias, dn_norm,
           ret_decay, ret_norm, w_out, norm2, router_w, router_b, w_gate, b_gate, w_up, b_up,
           w_down, b_down, ple_proj, ple_gate, ple_norm, final_norm):
    proj = _in_proj(cfg, x, norm1[layer], _reorder_w_in(w_in[layer]), tabs[0], tabs[1], 512)
    qkv, gates = _dn_prep(cfg, proj, conv_w[layer], dn_a_log[layer], dn_dt_bias[layer], 512)
    dn_f, dn_b = _dn_scan(cfg, qkv, gates, 256)
    ret_f, ret_b = _ret_scan(cfg, proj, _ret_tables(ret_decay[layer]), 256)
    att = _attention(cfg, proj)
    x1, h2, rt, cnt = _post(cfg, dn_f, dn_b, ret_f, ret_b, att, proj, x, w_out[layer],
                            dn_norm[layer], ret_norm[layer], norm2[layer], router_w[layer],
                            router_b[layer])
    n_rows = _moe_rows(cfg)
    lstart, gdest, block_exp, nvalid = _route_tables(cnt[:, 0, :], n_rows)
    xin = _dispatch(cfg, h2, rt, lstart, gdest, jnp.zeros((n_rows, D_MODEL), F32))
    y = _experts(layer, xin, block_exp, nvalid, w_gate, b_gate, w_up, b_up, w_down, b_down)
    return _combine(cfg, final, rt, lstart, gdest, x1, p_l, ple_gate[layer], ple_proj[layer],
                    ple_norm[layer], final_norm, y)


def _trunk(cfg, x, p, final_norm, params):
    tabs = _rope_tables(cfg.smax)
    depth = p.shape[0]
    for layer in range(depth):
        x = _layer(cfg, layer, layer == depth - 1, x, p[layer], tabs, *params, final_norm)
    return x


def kernel(x_prompt, x_sample, p_prompt, p_sample, norm1, w_in, conv_w, dn_a_log, dn_dt_bias,
           dn_norm, ret_decay, ret_norm, w_out, norm2, router_w, router_b, w_gate, b_gate,
           w_up, b_up, w_down, b_down, ple_proj, ple_gate, ple_norm, final_norm):
    b0, s0, d = x_prompt.shape
    b1, s1, _ = x_sample.shape
    cfg = Cfg(b0, s0, b1, s1)
    depth = p_prompt.shape[0]
    x = jnp.concatenate([x_prompt.reshape(cfg.t0, d), x_sample.reshape(cfg.t1, d)], axis=0)
    p = jnp.concatenate([p_prompt.reshape(depth, cfg.t0, PLE_DIM),
                         p_sample.reshape(depth, cfg.t1, PLE_DIM)], axis=1)
    params = (norm1, w_in, conv_w, dn_a_log, dn_dt_bias, dn_norm, ret_decay, ret_norm, w_out,
              norm2, router_w, router_b, w_gate, b_gate, w_up, b_up, w_down, b_down,
              ple_proj, ple_gate, ple_norm)
    y = _trunk(cfg, x, p, final_norm, params)
    return (y[:cfg.t0].reshape(b0, s0, d), y[cfg.t0:].reshape(b1, s1, d))
```

```python
import functools
import math

import jax
import jax.numpy as jnp
import numpy as np
from jax import lax
from jax.experimental import pallas as pl
from jax.experimental.pallas import tpu as pltpu

F32 = jnp.float32
BF16 = jnp.bfloat16
I32 = jnp.int32

D_MODEL = 1024
HEAD_DIM = 64
DN_HEADS = 4
RET_HEADS = 4
ATT_HEADS = 8
DN_WIDTH = 256
RET_WIDTH = 256
ATT_WIDTH = 512
CONV_W = 4
RET_THETA = 10000.0
ROPE_THETA = 500000.0
ROPE_DIM = HEAD_DIM // 4
DILATED_PATTERNS = ((128, 1), (512, 4), (2048, 16))
ATT_HALF = 64
N_EXPERTS = 32
TOP_K = 4
SWIGLU_ALPHA = 1.702
SWIGLU_LIMIT = 7.0
PLE_DIM = 256
NORM_EPS = 1e-6
NEG_BIG = -1e30

LANES = 128
SUBLANES = 8
VMEM_LIMIT = 56 * 1024 * 1024

COL_DN_QKV = 0
COL_DN_Z = 768
COL_RQ = 1024
COL_RK = 1280
COL_RV = 1536
COL_RG = 1792
COL_AQ = 2048
COL_AK = 2560
COL_AV = 3072
COL_AB = 3584
PROJ_W = 3840

CHUNK = 64
CAT = 4 * CHUNK


class Cfg:
    def __init__(self, b0, s0, b1, s1):
        self.b0, self.s0, self.b1, self.s1 = b0, s0, b1, s1
        self.t0 = b0 * s0
        self.t1 = b1 * s1
        self.t = self.t0 + self.t1
        self.smax = max(s0, s1)

    def pos(self, row):
        return jnp.where(row < self.t0, row % self.s0, (row - self.t0) % self.s1)

    def seq_len(self, row):
        return jnp.where(row < self.t0, self.s0, self.s1)


def _cparams(n_axes=1, sem=None):
    return pltpu.CompilerParams(
        dimension_semantics=tuple(sem or ("arbitrary",) * n_axes),
        vmem_limit_bytes=VMEM_LIMIT)


def _group_specs(cfg, tm, width, n_parts, lead=None):
    pre_b = () if lead is None else (None,)
    pre_i = () if lead is None else (lead,)
    if n_parts == 1:
        return [pl.BlockSpec(pre_b + (tm, width), lambda i, *_: pre_i + (i, 0))]
    n0 = cfg.t0 // tm
    return [pl.BlockSpec(pre_b + (tm, width), lambda i, *_: pre_i + (jnp.minimum(i, n0 - 1), 0)),
            pl.BlockSpec(pre_b + (tm, width), lambda i, *_: pre_i + (jnp.maximum(i - n0, 0), 0))]


def _group_load(cfg, tm, refs):
    if len(refs) == 1:
        return refs[0][...]
    return jnp.where(pl.program_id(0) < cfg.t0 // tm, refs[0][...], refs[1][...])


def _split_bf16(x):
    hi = x.astype(BF16)
    lo = (x - hi.astype(F32)).astype(BF16)
    return hi, lo


def _dot(a, b):
    return jnp.dot(a, b, preferred_element_type=F32)


def _dot_x2(a, b_exact):
    hi, lo = _split_bf16(a)
    return _dot(hi, b_exact) + _dot(lo, b_exact)


def _dot_lx2(a_exact, b):
    hi, lo = _split_bf16(b)
    return _dot(a_exact, hi) + _dot(a_exact, lo)


def _dot_x3(a, b):
    ah, al = _split_bf16(a)
    bh, bl = _split_bf16(b)
    return _dot(ah, bh) + (_dot(al, bh) + _dot(ah, bl))


def _sigmoid(x):
    return 1.0 / (1.0 + jnp.exp(-x))


def _silu(x):
    return x * _sigmoid(x)


def _head_sum_matrix(width):
    r = lax.broadcasted_iota(I32, (width, width), 0) // HEAD_DIM
    c = lax.broadcasted_iota(I32, (width, width), 1) // HEAD_DIM
    return jnp.where(r == c, 1.0, 0.0).astype(BF16)


_SECTIONS = tuple((c, 1 if c in (COL_RQ, COL_RK) else 2 if COL_AQ <= c < COL_AV else 0)
                  for c in range(0, PROJ_W, 256))


def _rope_tables(smax):
    pos = jnp.arange(smax, dtype=F32)[:, None]
    d = jnp.arange(256) % HEAD_DIM

    def build(inv, half):
        rot = 2 * half
        ang = pos * inv[d % half][None, :]
        cos, sin = jnp.cos(ang), jnp.sin(ang)
        in_rot = (d < rot)[None, :]
        first = (d < half)[None, :]
        c = jnp.where(in_rot, cos, 1.0)
        s_plus = jnp.where(in_rot & ~first, sin, 0.0)
        s_minus = jnp.where(first, -sin, 0.0)
        return jnp.stack([c, s_plus, s_minus])

    inv_ret = 1.0 / jnp.power(jnp.float32(RET_THETA), jnp.linspace(0.0, 1.0, HEAD_DIM // 2, dtype=F32))
    inv_att = jnp.power(jnp.float32(ROPE_THETA), -jnp.arange(0, ROPE_DIM, 2, dtype=F32) / ROPE_DIM)
    return build(inv_ret, HEAD_DIM // 2), build(inv_att, ROPE_DIM // 2)


def _in_proj_kernel(cfg, tm, nx, *refs):
    g_ref, w_ref, tr_ref, ta_ref, o_ref = refs[nx:]
    x = _group_load(cfg, tm, refs[:nx])
    y = x * lax.rsqrt(jnp.mean(x * x, axis=-1, keepdims=True) + NORM_EPS)
    h = (y * g_ref[...]).astype(BF16)
    for col, kind in _SECTIONS:
        acc = _dot(h, w_ref[:, col:col + 256])
        if kind:
            t_ref, half = (tr_ref, HEAD_DIM // 2) if kind == 1 else (ta_ref, ROPE_DIM // 2)
            acc = (acc * t_ref[0] + pltpu.roll(acc, half, 1) * t_ref[1]
                   + pltpu.roll(acc, 256 - half, 1) * t_ref[2])
        o_ref[:, col:col + 256] = acc


def _in_proj(cfg, xs, g1, w_bf16, tab_ret, tab_att, tm):
    n = cfg.t // tm

    def tab_map(i):
        return (0, cfg.pos(i * tm) // tm, 0)

    return pl.pallas_call(
        functools.partial(_in_proj_kernel, cfg, tm, len(xs)),
        grid=(n,),
        in_specs=_group_specs(cfg, tm, D_MODEL, len(xs)) + [
            pl.BlockSpec((1, D_MODEL), lambda i: (0, 0)),
            pl.BlockSpec((D_MODEL, PROJ_W), lambda i: (0, 0)),
            pl.BlockSpec((3, tm, 256), tab_map),
            pl.BlockSpec((3, tm, 256), tab_map),
        ],
        out_specs=pl.BlockSpec((tm, PROJ_W), lambda i: (i, 0)),
        out_shape=jax.ShapeDtypeStruct((cfg.t, PROJ_W), F32),
        compiler_params=_cparams(),
        name="in_proj",
    )(*xs, g1.reshape(1, D_MODEL), w_bf16, tab_ret, tab_att)


def _reorder_w_in(w_in):
    ab = w_in[:, 1024:1040]
    rest = w_in[:, 1040:]
    pad = jnp.zeros((w_in.shape[0], PROJ_W - 3600), w_in.dtype)
    return jnp.concatenate([w_in[:, :1024], rest, ab, pad], axis=1).astype(BF16)


def _dn_prep_kernel(cfg, tm, cur_ref, prev_ref, next_ref, ab_ref, cw_ref, ga_ref, gb_ref,
                    qkv_ref, gate_ref, ext_ref):
    i = pl.program_id(0)
    pos = cfg.pos(i * tm)
    first = pos == 0
    last = pos + tm == cfg.seq_len(i * tm)
    ext_ref[0:8, :] = jnp.where(first, 0.0, prev_ref[...])
    ext_ref[8:8 + tm, :] = cur_ref[...]
    ext_ref[8 + tm:16 + tm, :] = jnp.where(last, 0.0, next_ref[...])
    y = cw_ref[0:1, :] * ext_ref[6:6 + tm, :]
    for j in range(1, CONV_W):
        y = y + cw_ref[j:j + 1, :] * ext_ref[6 + j:6 + j + tm, :]
    y = _silu(y)
    hsum = _head_sum_matrix(DN_WIDTH)
    q = y[:, 0:DN_WIDTH]
    k = y[:, DN_WIDTH:2 * DN_WIDTH]
    qkv_ref[:, 0:DN_WIDTH] = q * lax.rsqrt(_dot_x2(q * q, hsum) + NORM_EPS) * (HEAD_DIM ** -0.5)
    qkv_ref[:, DN_WIDTH:2 * DN_WIDTH] = k * lax.rsqrt(_dot_x2(k * k, hsum) + NORM_EPS)
    qkv_ref[:, 2 * DN_WIDTH:] = y[:, 2 * DN_WIDTH:]
    ab = ab_ref[:, 0:LANES]
    xa = ab + gb_ref[...]
    softplus = jnp.maximum(xa, 0.0) + jnp.log(1.0 + jnp.exp(-jnp.abs(xa)))
    lane = lax.broadcasted_iota(I32, ab.shape, 1)
    gate_ref[...] = jnp.where(lane < 2 * DN_HEADS, ga_ref[...] * softplus, _sigmoid(ab))


def _dn_prep(cfg, proj, conv_w, a_log, dt_bias, tm):
    n = cfg.t // tm
    w3 = 3 * DN_WIDTH
    nh = tm // 8
    last8 = cfg.t // 8 - 1
    neg_a = jnp.zeros((1, LANES), F32).at[0, :2 * DN_HEADS].set(-jnp.exp(a_log.reshape(-1)))
    dtb = jnp.zeros((1, LANES), F32).at[0, :2 * DN_HEADS].set(dt_bias.reshape(-1))
    return pl.pallas_call(
        functools.partial(_dn_prep_kernel, cfg, tm),
        grid=(n,),
        in_specs=[
            pl.BlockSpec((tm, w3), lambda i: (i, 0)),
            pl.BlockSpec((8, w3), lambda i: (jnp.maximum(i * nh - 1, 0), 0)),
            pl.BlockSpec((8, w3), lambda i: (jnp.minimum((i + 1) * nh, last8), 0)),
            pl.BlockSpec((tm, 256), lambda i: (i, COL_AB // 256)),
            pl.BlockSpec((CONV_W, w3), lambda i: (0, 0)),
            pl.BlockSpec((1, LANES), lambda i: (0, 0)),
            pl.BlockSpec((1, LANES), lambda i: (0, 0)),
        ],
        out_specs=[
            pl.BlockSpec((tm, w3), lambda i: (i, 0)),
            pl.BlockSpec((tm, LANES), lambda i: (i, 0)),
        ],
        out_shape=[jax.ShapeDtypeStruct((cfg.t, w3), F32),
                   jax.ShapeDtypeStruct((cfg.t, LANES), F32)],
        scratch_shapes=[pltpu.VMEM((tm + 16, w3), F32)],
        compiler_params=_cparams(),
        name="dn_prep",
    )(proj, proj, proj, proj, conv_w, neg_a, dtb)


def _bd_mask():
    r = lax.broadcasted_iota(I32, (CAT, CAT), 0) // CHUNK
    c = lax.broadcasted_iota(I32, (CAT, CAT), 1) // CHUNK
    return r == c


def _bd(x, bdmask):
    xb = x.astype(BF16)
    return jnp.where(bdmask, jnp.concatenate([xb, xb, xb, xb], axis=0), jnp.zeros((), BF16))


def _dot_t0(a, b):
    return lax.dot_general(a, b, (((0,), (0,)), ((), ())), preferred_element_type=F32)


def _dot_t1(a, b):
    return lax.dot_general(a, b, (((1,), (1,)), ((), ())), preferred_element_type=F32)


def _chunk_ij(rows):
    i = lax.broadcasted_iota(I32, (rows, CAT), 0) % CHUNK
    j = lax.broadcasted_iota(I32, (rows, CAT), 1) % CHUNK
    return i, j


def _segments(cfg, tm, want):
    for nseg in (want, 2, 1):
        seg = cfg.t // nseg
        if cfg.t % nseg == 0 and seg % tm == 0 and all(
                (s * seg) % cfg.s0 == 0 if s * seg <= cfg.t0 else (s * seg - cfg.t0) % cfg.s1 == 0
                for s in range(nseg)):
            return nseg
    return 1


def _stream_rows(nseg, per, tm):
    fns = []
    for s in range(nseg):
        fns.append(lambda i, s=s: (s * per + i) * tm)
        fns.append(lambda i, s=s: (s * per + per - 1 - i) * tm)
    return fns


def _stream_views(nseg, fwd_ref, bwd_ref):
    out = []
    for s in range(nseg):
        out += [fwd_ref.at[s], bwd_ref.at[s]]
    return out


def _dn_streams(revs, qkv_refs, gate_refs, o_refs, s_refs, tm):
    nc = tm // CHUNK
    bdmask = _bd_mask()
    i, j = _chunk_ij(tm)
    ic, jc = _chunk_ij(CHUNK)
    eye = jnp.where(ic == jc, 1.0, 0.0)
    l = lax.broadcasted_iota(I32, (LANES, 2 * CAT), 0)
    c = lax.broadcasted_iota(I32, (LANES, 2 * CAT), 1)
    ti = lax.broadcasted_iota(I32, (tm, tm), 0)
    tk = lax.broadcasted_iota(I32, (tm, tm), 1)
    same = (ti // CHUNK) == (tk // CHUNK)
    consts = {}
    for rev in sorted(set(revs)):
        base = (4 if rev else 0) + jnp.where(c >= CAT, 2 * DN_HEADS, 0)
        consts[rev] = dict(
            sel=jnp.where(l == base + (c % CAT) // CHUNK, 1.0, 0.0).astype(BF16),
            tri=jnp.where(same & ((tk >= ti) if rev else (tk <= ti)), 1.0, 0.0).astype(BF16),
            between=(i < j) if rev else (i > j),
            incl=(i <= j) if rev else (i >= j),
            strict=(ic < jc) if rev else (ic > jc))

    st = []
    for rev, qkv_ref, gate_ref in zip(revs, qkv_refs, gate_refs):
        cs = consts[rev]
        gb = _dot_x2(gate_ref[...], cs["sel"])
        g_cat, beta = gb[:, :CAT], gb[:, CAT:]
        gc = _dot_lx2(cs["tri"], g_cat)
        diff = _dot_lx2(cs["tri"], jnp.where(cs["between"], g_cat, 0.0))
        st.append(dict(rev=rev, beta=beta, gc=gc, egc=jnp.exp(gc),
                       decay=jnp.where(cs["incl"], jnp.exp(diff), 0.0),
                       q=qkv_ref[:, 0:CAT], k=qkv_ref[:, CAT:2 * CAT], v=qkv_ref[:, 2 * CAT:3 * CAT]))

    chains = [(si, ci) for si in range(len(st)) for ci in range(nc)]
    attn, amat, tinv = {}, {}, {}
    for si, ci in chains:
        s_, r = st[si], slice(ci * CHUNK, (ci + 1) * CHUNK)
        qk_kk = _dot_t1(jnp.concatenate([s_["q"][r], s_["k"][r]], axis=0).astype(BF16),
                        _bd(s_["k"][r], bdmask))
        attn[si, ci] = qk_kk[:CHUNK] * s_["decay"][r]
        a = jnp.where(consts[s_["rev"]]["strict"], qk_kk[CHUNK:] * s_["beta"][r] * s_["decay"][r], 0.0)
        amat[si, ci] = a
        tinv[si, ci] = eye - jnp.where((ic >> 1) == (jc >> 1), a, 0.0)
    for lvl in range(2, 7):
        omask = ((ic >> lvl) == (jc >> lvl)) & ((ic >> (lvl - 1)) != (jc >> (lvl - 1)))
        xs = {key: _dot(tinv[key].astype(BF16), _bd(jnp.where(omask, amat[key], 0.0), bdmask))
              for key in chains}
        for key in chains:
            tinv[key] = tinv[key] - _dot(xs[key].astype(BF16), _bd(tinv[key], bdmask))

    states = [s_ref[...] for s_ref in s_refs]
    for step in range(nc):
        idx = [(nc - 1 - step) if s_["rev"] else step for s_ in st]
        rs = [slice(ci * CHUNK, (ci + 1) * CHUNK) for ci in idx]
        x1s = [_dot(jnp.concatenate([s_["k"][r] * s_["beta"][r] * s_["egc"][r], s_["q"][r] * s_["egc"][r]],
                                    axis=0).astype(BF16), state.astype(BF16))
               for s_, r, state in zip(st, rs, states)]
        vns = [_dot(tinv[si, ci].astype(BF16), _bd(s_["v"][r] * s_["beta"][r] - x1[:CHUNK], bdmask))
               for si, (s_, r, ci, x1) in enumerate(zip(st, rs, idx, x1s))]
        for si, (s_, r, ci, x1, vn, o_ref) in enumerate(zip(st, rs, idx, x1s, vns, o_refs)):
            o_ref[r, :] = x1[CHUNK:] + _dot(attn[si, ci].astype(BF16), _bd(vn, bdmask))
            gcc = s_["gc"][r]
            g_last = gcc[0:1] if s_["rev"] else gcc[CHUNK - 1:CHUNK]
            k_dec = s_["k"][r] * jnp.exp(g_last - gcc)
            states[si] = states[si] * jnp.exp(g_last) + jnp.where(
                bdmask, _dot_t0(k_dec.astype(BF16), vn.astype(BF16)), 0.0)
    for s_ref, state in zip(s_refs, states):
        s_ref[...] = state


def _reset_states(cfg, tm, row_fns, s_refs):
    i = pl.program_id(0)
    for si, (row_fn, s_ref) in enumerate(zip(row_fns, s_refs)):
        row = row_fn(i)
        edge = (cfg.pos(row) + tm == cfg.seq_len(row)) if si % 2 else (cfg.pos(row) == 0)

        @pl.when(edge)
        def _(s_ref=s_ref):
            s_ref[...] = jnp.zeros_like(s_ref)


def _dn_scan_kernel(cfg, tm, nseg, per, qf_ref, gf_ref, qb_ref, gb_ref, of_ref, ob_ref, *s_refs):
    _reset_states(cfg, tm, _stream_rows(nseg, per, tm), s_refs)
    _dn_streams([si % 2 == 1 for si in range(2 * nseg)], _stream_views(nseg, qf_ref, qb_ref),
                _stream_views(nseg, gf_ref, gb_ref), _stream_views(nseg, of_ref, ob_ref), s_refs, tm)


def _scan_specs(nseg, per, tm, width, col_block=0):
    return (pl.BlockSpec((nseg, tm, width), lambda i: (0, i, col_block)),
            pl.BlockSpec((nseg, tm, width), lambda i: (0, per - 1 - i, col_block)))


def _dn_scan(cfg, qkv, gate, tm, want_segments=2):
    nseg = _segments(cfg, tm, want_segments)
    per = cfg.t // nseg // tm
    w3 = 3 * DN_WIDTH
    qf, qb = _scan_specs(nseg, per, tm, w3)
    gf, gb = _scan_specs(nseg, per, tm, LANES)
    of, ob = _scan_specs(nseg, per, tm, DN_WIDTH)
    qkv3 = qkv.reshape(nseg, cfg.t // nseg, w3)
    gate3 = gate.reshape(nseg, cfg.t // nseg, LANES)
    o_f, o_b = pl.pallas_call(
        functools.partial(_dn_scan_kernel, cfg, tm, nseg, per),
        grid=(per,),
        in_specs=[qf, gf, qb, gb],
        out_specs=[of, ob],
        out_shape=[jax.ShapeDtypeStruct((nseg, cfg.t // nseg, DN_WIDTH), F32)] * 2,
        scratch_shapes=[pltpu.VMEM((CAT, CAT), F32)] * (2 * nseg),
        compiler_params=_cparams(),
        name="dn_scan",
    )(qkv3, gate3, qkv3, gate3)
    return o_f.reshape(cfg.t, DN_WIDTH), o_b.reshape(cfg.t, DN_WIDTH)


def _ret_tables(decay_logit):
    lg = jnp.log(jax.nn.sigmoid(decay_logit.astype(F32)))
    i = jnp.arange(CHUNK, dtype=F32)[:, None]
    j = (jnp.arange(CAT) % CHUNK).astype(F32)[None, :]
    scale = HEAD_DIM ** -0.5
    out = []
    for d in range(2):
        lgl = jnp.repeat(lg[d], CHUNK)[None, :]
        rel = (j - i) if d else (i - j)
        dmat = jnp.where(rel >= 0, jnp.exp(jnp.where(rel >= 0, rel, 0.0) * lgl), 0.0) * scale
        qdec = jnp.exp(((CHUNK - i) if d else (i + 1.0)) * lgl)
        kdec = jnp.exp((i if d else (CHUNK - 1.0 - i)) * lgl) * scale
        cd = jnp.broadcast_to(jnp.exp(CHUNK * lgl), (CHUNK, CAT))
        out.append(jnp.stack([dmat, qdec, kdec, cd]))
    return jnp.stack(out)


def _ret_stream(rev, q_ref, k_ref, v_ref, tab_ref, o_ref, s_ref, tm):
    nc = tm // CHUNK
    bdmask = _bd_mask()
    d = 1 if rev else 0
    dmat, qdec, kdec, cd = tab_ref[d, 0], tab_ref[d, 1], tab_ref[d, 2], tab_ref[d, 3]
    order = range(nc - 1, -1, -1) if rev else range(nc)
    s = s_ref[...]
    for cidx in order:
        r = slice(cidx * CHUNK, (cidx + 1) * CHUNK)
        qc, kc, vc = q_ref[r, :], k_ref[r, :], v_ref[r, :]
        scores = _dot_t1(qc.astype(BF16), _bd(kc, bdmask)) * dmat
        inner = _dot(scores.astype(BF16), _bd(vc, bdmask))
        cross = _dot((qc * qdec).astype(BF16), s.astype(BF16))
        o_ref[r, :] = inner + cross
        kv = _dot_t0((kc * kdec).astype(BF16), vc.astype(BF16))
        s = s * cd[0:1] + jnp.where(bdmask, kv, 0.0)
    s_ref[...] = s


def _ret_scan_kernel(cfg, tm, n, qf, kf, vf, qb, kb, vb, tab_ref, of_ref, ob_ref, sf_ref, sb_ref):
    i = pl.program_id(0)
    row_f = i * tm
    row_b = (n - 1 - i) * tm

    @pl.when(cfg.pos(row_f) == 0)
    def _():
        sf_ref[...] = jnp.zeros_like(sf_ref)

    @pl.when(cfg.pos(row_b) + tm == cfg.seq_len(row_b))
    def _():
        sb_ref[...] = jnp.zeros_like(sb_ref)

    _ret_stream(False, qf, kf, vf, tab_ref, of_ref, sf_ref, tm)
    _ret_stream(True, qb, kb, vb, tab_ref, ob_ref, sb_ref, tm)


def _ret_scan(cfg, proj, tables, tm):
    n = cfg.t // tm

    def spec(col, rev):
        cb = col // 256
        return pl.BlockSpec((tm, 256), (lambda i: (n - 1 - i, cb)) if rev else (lambda i: (i, cb)))

    return pl.pallas_call(
        functools.partial(_ret_scan_kernel, cfg, tm, n),
        grid=(n,),
        in_specs=[spec(COL_RQ, False), spec(COL_RK, False), spec(COL_RV, False),
                  spec(COL_RQ, True), spec(COL_RK, True), spec(COL_RV, True),
                  pl.BlockSpec((2, 4, CHUNK, CAT), lambda i: (0, 0, 0, 0))],
        out_specs=[pl.BlockSpec((tm, RET_WIDTH), lambda i: (i, 0)),
                   pl.BlockSpec((tm, RET_WIDTH), lambda i: (n - 1 - i, 0))],
        out_shape=[jax.ShapeDtypeStruct((cfg.t, RET_WIDTH), F32)] * 2,
        scratch_shapes=[pltpu.VMEM((CAT, CAT), F32)] * 2,
        compiler_params=_cparams(),
        name="ret_scan",
    )(proj, proj, proj, proj, proj, proj, tables)


ATT_TQ = 2048
ATT_HALO = 1024
ATT_QB = 128


def _ds(start, size, stride):
    return pl.ds(start, size) if stride == 1 else pl.ds(start, size, stride=stride)


ATT_NB = 4


def _att_kernel(cfg, q_ref, kp_ref, kc_ref, kn_ref, vp_ref, vc_ref, vn_ref, o_ref,
                kbuf, vbuf, bias_sc, m_sc, l_sc, acc_sc):
    tq, halo, qb = ATT_TQ, ATT_HALO, ATT_QB
    kw = qb + 2 * ATT_HALF
    i = pl.program_id(1)
    pos0 = cfg.pos(i * tq)
    slen = cfg.seq_len(i * tq)
    kbuf[0:halo, :] = kp_ref[...]
    kbuf[halo:halo + tq, :] = kc_ref[...]
    kbuf[halo + tq:, :] = kn_ref[...]
    vbuf[0:halo, :] = vp_ref[...]
    vbuf[halo:halo + tq, :] = vc_ref[...]
    vbuf[halo + tq:, :] = vn_ref[...]
    head0 = lax.broadcasted_iota(I32, (qb, LANES), 1) < HEAD_DIM
    qi = lax.broadcasted_iota(I32, (qb, kw), 0)
    kj = lax.broadcasted_iota(I32, (qb, kw), 1)
    bias_sc[...] = jnp.where((kj - qi >= 0) & (kj - qi <= 2 * ATT_HALF), 0.0, NEG_BIG)
    krow = lax.broadcasted_iota(I32, (1, kw), 1)
    qscale = HEAD_DIM ** -0.5 * math.log2(math.e)
    ones_v = jnp.ones((kw, LANES), BF16)

    for p, (_, dil) in enumerate(DILATED_PATTERNS):
        def body(it, carry, p=p, dil=dil):
            where_q, where_k, scores = [], [], []
            for jj in range(ATT_NB):
                u = it * ATT_NB + jj
                r = u % dil
                b = u // dil
                qs = r + dil * qb * b
                ks = halo + r + dil * (qb * b - ATT_HALF)
                where_q.append(qs)
                where_k.append(ks)
                q = q_ref[_ds(qs, qb, dil), :] * qscale
                k = kbuf[_ds(ks, kw, dil), :].astype(BF16)
                kpos = pos0 // dil + qb * b - ATT_HALF + krow
                valid = jnp.where((kpos >= 0) & (kpos < slen // dil), 0.0, NEG_BIG)
                for h in range(2):
                    qh = jnp.where(head0 if h == 0 else ~head0, q, 0.0).astype(BF16)
                    scores.append(_dot_t1(qh, k) + bias_sc[...] + valid)
            ms = [jnp.max(s, axis=-1, keepdims=True) for s in scores]
            ps = [jnp.exp2(s - m).astype(BF16) for s, m in zip(scores, ms)]
            for jj in range(ATT_NB):
                v = vbuf[_ds(where_k[jj], kw, dil), :].astype(BF16)
                vext = jnp.concatenate([v, ones_v], axis=1)
                a0 = _dot(ps[2 * jj], vext)
                a1 = _dot(ps[2 * jj + 1], vext)
                rows = _ds(where_q[jj], qb, dil)
                m_sc[p, rows, :] = jnp.where(head0, ms[2 * jj], ms[2 * jj + 1])
                l_sc[p, rows, :] = jnp.where(head0, a0[:, LANES:], a1[:, LANES:])
                acc_sc[p, rows, :] = jnp.where(head0, a0[:, :LANES], a1[:, :LANES])
            return carry

        lax.fori_loop(0, tq // qb // ATT_NB, body, 0)

    m_all = [m_sc[p] for p in range(3)]
    mx = jnp.maximum(jnp.maximum(m_all[0], m_all[1]), m_all[2])
    num = jnp.zeros((tq, LANES), F32)
    den = jnp.zeros((tq, LANES), F32)
    for p in range(3):
        w = jnp.exp2(m_all[p] - mx)
        num = num + w * acc_sc[p]
        den = den + w * l_sc[p]
    o_ref[...] = num / den


def _attention(cfg, proj):
    tq, halo = ATT_TQ, ATT_HALO
    n = cfg.t // tq
    per = tq // halo
    nh = cfg.t // halo

    def cur(col):
        return pl.BlockSpec((tq, LANES), lambda hp, i: (i, col // LANES + hp))

    def prev(col):
        return pl.BlockSpec((halo, LANES), lambda hp, i: (jnp.maximum(i * per - 1, 0), col // LANES + hp))

    def nxt(col):
        return pl.BlockSpec((halo, LANES), lambda hp, i: (jnp.minimum((i + 1) * per, nh - 1), col // LANES + hp))

    return pl.pallas_call(
        functools.partial(_att_kernel, cfg),
        grid=(ATT_HEADS // 2, n),
        in_specs=[cur(COL_AQ), prev(COL_AK), cur(COL_AK), nxt(COL_AK),
                  prev(COL_AV), cur(COL_AV), nxt(COL_AV)],
        out_specs=pl.BlockSpec((tq, LANES), lambda hp, i: (i, hp)),
        out_shape=jax.ShapeDtypeStruct((cfg.t, ATT_WIDTH), F32),
        scratch_shapes=[pltpu.VMEM((tq + 2 * halo, LANES), F32)] * 2
                       + [pltpu.VMEM((ATT_QB, ATT_QB + 2 * ATT_HALF), F32)]
                       + [pltpu.VMEM((3, tq, LANES), F32)] * 3,
        compiler_params=_cparams(2),
        name="attention",
    )(proj, proj, proj, proj, proj, proj, proj)


MOE_TM = 512
MOE_R = 4 * MOE_TM + 256
MOE_NG = MOE_R // SUBLANES
MOE_XB = 512
MOE_DUMP = -(-MOE_R // MOE_XB) * MOE_XB
RT_ID, RT_RANK, RT_GATE = 0, 4, 8


def _rms(x, g):
    return x * lax.rsqrt(jnp.mean(x * x, axis=-1, keepdims=True) + NORM_EPS) * g


def _post_kernel(cfg, nx, dnf, dnb, z_ref, rtf, rtb, rg_ref, att_ref, *refs):
    (wout_ref, dnn_ref, rtn_ref, n2_ref, rw_ref, rb_ref, ltri_ref,
     x1_ref, h2_ref, rt_ref, cnt_ref) = refs[nx:]
    hsum = _head_sum_matrix(DN_WIDTH)
    inv = 1.0 / HEAD_DIM
    o = dnf[...] + dnb[...]
    dn = o * lax.rsqrt(_dot_x2(o * o, hsum) * inv + NORM_EPS) * dnn_ref[...] * _silu(z_ref[...])
    o = rtf[...] + rtb[...]
    oc = o - _dot_x2(o, hsum) * inv
    ret = oc * lax.rsqrt(_dot_x2(oc * oc, hsum) * inv + NORM_EPS) * rtn_ref[...] * _silu(rg_ref[...])
    mixed = (_dot(dn.astype(BF16), wout_ref[0:256, :]) + _dot(ret.astype(BF16), wout_ref[256:512, :])
             + _dot(att_ref[...].astype(BF16), wout_ref[512:1024, :]))
    x1 = _group_load(cfg, MOE_TM, refs[:nx]) + mixed
    x1_ref[...] = x1
    h2 = _rms(x1, n2_ref[...])
    h2_ref[...] = h2.astype(BF16)
    hh, hl = _split_bf16(h2)
    logits = _dot(hh, rw_ref[0]) + (_dot(hl, rw_ref[0]) + _dot(hh, rw_ref[1])) + rb_ref[...]
    tm = logits.shape[0]
    lane = lax.broadcasted_iota(I32, (tm, LANES), 1)
    lane_f = lane.astype(F32)
    work = logits
    vals, ids = [], []
    for _ in range(TOP_K):
        m = jnp.max(work, axis=-1, keepdims=True)
        idx = jnp.min(jnp.where(work == m, lane_f, float(LANES)), axis=-1, keepdims=True).astype(I32)
        vals.append(m)
        ids.append(idx)
        work = jnp.where(lane == idx, -3e38, work)
    es = [jnp.exp(v - vals[0]) for v in vals]
    tot = es[0] + es[1] + es[2] + es[3]
    onehots = [jnp.where(lane == idx, 1.0, 0.0) for idx in ids]
    msum = onehots[0] + onehots[1] + onehots[2] + onehots[3]
    before = _dot(ltri_ref[...], msum.astype(BF16))
    rec = jnp.zeros((tm, LANES), I32)
    for k in range(TOP_K):
        rank = jnp.sum(onehots[k] * before, axis=-1, keepdims=True).astype(I32)
        gate_bits = lax.bitcast_convert_type(es[k] / tot, I32)
        rec = jnp.where(lane == RT_ID + k, ids[k], rec)
        rec = jnp.where(lane == RT_RANK + k, rank, rec)
        rec = jnp.where(lane == RT_GATE + k, gate_bits, rec)
    rt_ref[...] = rec
    cnt_ref[0] = (before[tm - 1:tm, :] + msum[tm - 1:tm, :]).astype(I32)


def _post(cfg, dn_f, dn_b, ret_f, ret_b, att, proj, xs, w_out, dn_norm, ret_norm, norm2,
          router_w, router_b):
    tm = MOE_TM
    n = cfg.t // tm
    rw = jnp.zeros((D_MODEL, LANES), F32).at[:, :N_EXPERTS].set(router_w)
    rw_hi = rw.astype(BF16)
    rw2 = jnp.stack([rw_hi, (rw - rw_hi.astype(F32)).astype(BF16)])
    rb = jnp.full((1, LANES), NEG_BIG, F32).at[0, :N_EXPERTS].set(router_b)
    ltri = (jnp.arange(tm)[:, None] > jnp.arange(tm)[None, :]).astype(BF16)
    row = lambda w: pl.BlockSpec((tm, w), lambda i: (i, 0))
    col = lambda c: pl.BlockSpec((tm, 256), lambda i: (i, c // 256))
    const = lambda *s: pl.BlockSpec(s, lambda i: (0,) * len(s))
    return pl.pallas_call(
        functools.partial(_post_kernel, cfg, len(xs)),
        grid=(n,),
        in_specs=[row(256), row(256), col(COL_DN_Z), row(256), row(256), col(COL_RG), row(512)]
                 + _group_specs(cfg, tm, D_MODEL, len(xs))
                 + [const(D_MODEL, D_MODEL), const(1, 256), const(1, 256),
                    const(1, D_MODEL), const(2, D_MODEL, LANES), const(1, LANES), const(tm, tm)],
        out_specs=[row(D_MODEL), row(D_MODEL), row(LANES), pl.BlockSpec((1, 1, LANES), lambda i: (i, 0, 0))],
        out_shape=[jax.ShapeDtypeStruct((cfg.t, D_MODEL), F32),
                   jax.ShapeDtypeStruct((cfg.t, D_MODEL), BF16),
                   jax.ShapeDtypeStruct((cfg.t, LANES), I32),
                   jax.ShapeDtypeStruct((n, 1, LANES), I32)],
        compiler_params=_cparams(),
        name="post_route",
    )(dn_f, dn_b, proj, ret_f, ret_b, proj, att, *xs, w_out.astype(BF16),
      jnp.tile(dn_norm, DN_HEADS).reshape(1, 256), ret_norm.reshape(1, 256),
      norm2.reshape(1, D_MODEL), rw2, rb, ltri)


def _route_tables(cnt, n_rows):
    n = cnt.shape[0]
    c8 = (cnt[:, :N_EXPERTS] + 7) // 8 * 8
    lstart = jnp.cumsum(c8, axis=1) - c8
    tot = jnp.sum(c8, axis=0)
    region = (tot + MOE_XB - 1) // MOE_XB * MOE_XB
    pend = jnp.cumsum(region)
    gstart = (pend - region)[None, :] + jnp.cumsum(c8, axis=0) - c8
    lrow = jnp.arange(MOE_NG) * SUBLANES
    seg_end = (lstart + c8)[:, None, :]
    e = jnp.sum((lrow[None, :, None] >= seg_end).astype(I32), axis=2)
    hot = e[:, :, None] == jnp.arange(N_EXPERTS)[None, None, :]
    g = jnp.sum(jnp.where(hot, (gstart - lstart)[:, None, :], 0), axis=2) + lrow[None, :]
    dump = n_rows - MOE_DUMP + lrow
    gdest = jnp.where(e < N_EXPERTS, g, dump[None, :]).reshape(-1).astype(I32)
    n_blocks = n_rows // MOE_XB
    nvalid = (pend[-1] // MOE_XB).astype(I32)
    bstart = jnp.minimum(jnp.arange(n_blocks), nvalid - 1) * MOE_XB
    block_exp = jnp.minimum(jnp.sum((pend[None, :] <= bstart[:, None]).astype(I32), axis=1),
                            N_EXPERTS - 1).astype(I32)
    last_blk = jnp.where(region > 0, pend - MOE_XB, -1).astype(I32)
    lstart_p = jnp.zeros((n, 1, LANES), I32).at[:, 0, :N_EXPERTS].set(lstart.astype(I32))
    return lstart_p, gdest, block_exp, last_blk, nvalid.reshape(1)


def _moe_rows(cfg):
    n_tiles = cfg.t // MOE_TM
    worst = TOP_K * cfg.t + n_tiles * N_EXPERTS * 7 + N_EXPERTS * (MOE_XB - 1)
    return -(-worst // MOE_XB) * MOE_XB + MOE_DUMP


def _local_dest(rt, lstart_row):
    lane = lax.broadcasted_iota(I32, rt.shape, 1)
    ls = lstart_row.astype(F32)
    out = []
    for k in range(TOP_K):
        eid = rt[:, RT_ID + k:RT_ID + k + 1]
        base = jnp.sum(jnp.where(lane == eid, ls, 0.0), axis=-1, keepdims=True).astype(I32)
        out.append(base + rt[:, RT_RANK + k:RT_RANK + k + 1])
    return out


def _group_copy(src, dst, sem, s, d):
    return pltpu.make_async_copy(src.at[pl.ds(s, SUBLANES)], dst.at[pl.ds(d, SUBLANES)], sem)


def _dispatch_kernel(n_blocks, gd_ref, lb_ref, nv_ref, h2_ref, rt_ref, ls_ref, xin_ref, xs_ref, sem):
    i = pl.program_id(0)

    @pl.when(i == 0)
    def _():
        xs_ref[0:MOE_XB, :] = jnp.zeros((MOE_XB, D_MODEL), F32)

        def zero_copy(d):
            return pltpu.make_async_copy(xs_ref.at[pl.ds(0, MOE_XB)],
                                         xin_ref.at[pl.ds(pl.multiple_of(d, MOE_XB), MOE_XB)], sem)

        def per_expert(fn):
            def body(e, c):
                d = lb_ref[e]

                @pl.when(d >= 0)
                def _():
                    fn(zero_copy(d))
                return c
            lax.fori_loop(0, N_EXPERTS, body, 0)

        def per_tail(fn):
            def body(b, c):
                fn(zero_copy(b * MOE_XB))
                return c
            lax.fori_loop(nv_ref[0], n_blocks, body, 0)

        per_expert(lambda cp: cp.start())
        per_tail(lambda cp: cp.start())
        per_expert(lambda cp: cp.wait())
        per_tail(lambda cp: cp.wait())

    ld = _local_dest(rt_ref[...], ls_ref[0])
    slot = lax.broadcasted_iota(I32, (MOE_TM, MOE_R), 1)
    hit = (slot == ld[0]) | (slot == ld[1]) | (slot == ld[2]) | (slot == ld[3])
    pt = jnp.where(hit, 1.0, 0.0).astype(BF16)
    xs_ref[...] = _dot_t0(pt, h2_ref[...])

    def start(j, c):
        d = gd_ref[i * MOE_NG + j]
        _group_copy(xs_ref, xin_ref, sem, pl.multiple_of(j * SUBLANES, SUBLANES),
                    pl.multiple_of(d, SUBLANES)).start()
        return c

    lax.fori_loop(0, MOE_NG, start, 0)
    pltpu.make_async_copy(xs_ref, xin_ref.at[pl.ds(0, MOE_R)], sem).wait()


def _dispatch(cfg, h2, rt, lstart, gdest, last_blk, nvalid, n_rows):
    tm = MOE_TM
    n = cfg.t // tm
    grid_spec = pltpu.PrefetchScalarGridSpec(
        num_scalar_prefetch=3,
        grid=(n,),
        in_specs=[pl.BlockSpec((tm, D_MODEL), lambda i, *_: (i, 0)),
                  pl.BlockSpec((tm, LANES), lambda i, *_: (i, 0)),
                  pl.BlockSpec((1, 1, LANES), lambda i, *_: (i, 0, 0))],
        out_specs=pl.BlockSpec(memory_space=pl.ANY),
        scratch_shapes=[pltpu.VMEM((MOE_R, D_MODEL), F32), pltpu.SemaphoreType.DMA(())],
    )
    return pl.pallas_call(
        functools.partial(_dispatch_kernel, n_rows // MOE_XB),
        grid_spec=grid_spec,
        out_shape=jax.ShapeDtypeStruct((n_rows, D_MODEL), F32),
        compiler_params=_cparams(),
        name="moe_dispatch",
    )(gdest, last_blk, nvalid, h2, rt, lstart)


def _expert_kernel(layer, be_ref, nv_ref, x_ref, wg_ref, bg_ref, wu_ref, bu_ref, wd_ref, bd_ref,
                   y_ref, wgc, wuc, wdc):
    del layer
    b = pl.program_id(0)
    prev = be_ref[jnp.maximum(b - 1, 0)]

    @pl.when((b == 0) | (be_ref[b] != prev))
    def _():
        wgc[...] = wg_ref[...].astype(BF16)
        wuc[...] = wu_ref[...].astype(BF16)
        wdc[...] = wd_ref[...].astype(BF16)

    @pl.when(b < nv_ref[0])
    def _():
        xb = x_ref[...].astype(BF16)
        gt = jnp.minimum(_dot(xb, wgc[...]) + bg_ref[...], SWIGLU_LIMIT)
        up = jnp.clip(_dot(xb, wuc[...]) + bu_ref[...], -SWIGLU_LIMIT, SWIGLU_LIMIT)
        hid = (up + 1.0) * gt * _sigmoid(SWIGLU_ALPHA * gt)
        y_ref[...] = _dot(hid.astype(BF16), wdc[...]) + bd_ref[...]

    @pl.when(b >= nv_ref[0])
    def _():
        y_ref[...] = jnp.zeros_like(y_ref)


def _experts(layer, xin, block_exp, nvalid, w_gate, b_gate, w_up, b_up, w_down, b_down):
    n_blocks = xin.shape[0] // MOE_XB
    d = D_MODEL
    nl, ne = w_gate.shape[0], w_gate.shape[1]
    xmap = lambda b, be, nv: (jnp.minimum(b, nv[0] - 1), 0)
    wmap = lambda b, be, nv: (layer, be[b], 0, 0)
    wspec = pl.BlockSpec((None, None, d, d), wmap)
    bspec = pl.BlockSpec((None, None, 1, d), wmap)
    grid_spec = pltpu.PrefetchScalarGridSpec(
        num_scalar_prefetch=2,
        grid=(n_blocks,),
        in_specs=[pl.BlockSpec((MOE_XB, d), xmap), wspec, bspec, wspec, bspec, wspec, bspec],
        out_specs=pl.BlockSpec((MOE_XB, d), lambda b, be, nv: (b, 0)),
        scratch_shapes=[pltpu.VMEM((d, d), BF16)] * 3,
    )
    return pl.pallas_call(
        functools.partial(_expert_kernel, layer),
        grid_spec=grid_spec,
        out_shape=jax.ShapeDtypeStruct(xin.shape, F32),
        compiler_params=_cparams(),
        name="moe_experts",
    )(block_exp, nvalid, xin, w_gate, b_gate.reshape(nl, ne, 1, d), w_up, b_up.reshape(nl, ne, 1, d),
      w_down, b_down.reshape(nl, ne, 1, d))


def _combine_kernel(cfg, final, gd_ref, rt_ref, ls_ref, x1_ref, pa_ref, pb_ref, pg_ref, pp_ref,
                    pn_ref, fn_ref, y_ref, *refs):
    o_refs, (ys_ref, sem) = refs[:-2], refs[-2:]
    i = pl.program_id(0)

    def start(j, c):
        d = gd_ref[i * MOE_NG + j]
        _group_copy(y_ref, ys_ref, sem, pl.multiple_of(d, SUBLANES),
                    pl.multiple_of(j * SUBLANES, SUBLANES)).start()
        return c

    lax.fori_loop(0, MOE_NG, start, 0)
    rt = rt_ref[...]
    ld = _local_dest(rt, ls_ref[0])
    slot = lax.broadcasted_iota(I32, (MOE_TM, MOE_R), 1)
    ptg = jnp.zeros((MOE_TM, MOE_R), F32)
    for k in range(TOP_K):
        gate = lax.bitcast_convert_type(rt[:, RT_GATE + k:RT_GATE + k + 1], F32)
        ptg = jnp.where(slot == ld[k], gate, ptg)
    pltpu.make_async_copy(y_ref.at[pl.ds(0, MOE_R)], ys_ref, sem).wait()
    x2 = x1_ref[...] + _dot(ptg.astype(BF16), ys_ref[...].astype(BF16))
    gate = _sigmoid(_dot(x2.astype(BF16), pg_ref[...]))
    p = _group_load(cfg, MOE_TM, (pa_ref, pb_ref))
    e = _dot(p.astype(BF16), pp_ref[...]) * gate
    x3 = x2 + _rms(e, pn_ref[...])
    if not final:
        o_refs[0][...] = x3
    else:
        x3 = _rms(x3, fn_ref[...])
        is_prompt = i < cfg.t0 // MOE_TM

        @pl.when(is_prompt)
        def _():
            o_refs[0][...] = x3

        @pl.when(jnp.logical_not(is_prompt))
        def _():
            o_refs[1][...] = x3


def _combine(cfg, layer, final, rt, lstart, gdest, x1, p_prompt, p_sample, ple_gate, ple_proj,
             ple_norm, final_norm, y):
    tm = MOE_TM
    n = cfg.t // tm
    row = lambda w: pl.BlockSpec((tm, w), lambda i, gd: (i, 0))
    const = lambda *s: pl.BlockSpec(s, lambda i, gd: (0,) * len(s))
    if final:
        out_specs = _group_specs(cfg, tm, D_MODEL, 2)
        out_shape = [jax.ShapeDtypeStruct((cfg.t0, D_MODEL), F32),
                     jax.ShapeDtypeStruct((cfg.t1, D_MODEL), F32)]
    else:
        out_specs = [row(D_MODEL)]
        out_shape = [jax.ShapeDtypeStruct((cfg.t, D_MODEL), F32)]
    grid_spec = pltpu.PrefetchScalarGridSpec(
        num_scalar_prefetch=1,
        grid=(n,),
        in_specs=[row(LANES), pl.BlockSpec((1, 1, LANES), lambda i, gd: (i, 0, 0)), row(D_MODEL)]
                 + _group_specs(cfg, tm, PLE_DIM, 2, lead=layer)
                 + [const(D_MODEL, D_MODEL), const(PLE_DIM, D_MODEL),
                    const(1, D_MODEL), const(1, D_MODEL), pl.BlockSpec(memory_space=pl.ANY)],
        out_specs=out_specs,
        scratch_shapes=[pltpu.VMEM((MOE_R, D_MODEL), F32), pltpu.SemaphoreType.DMA(())],
    )
    outs = pl.pallas_call(
        functools.partial(_combine_kernel, cfg, final),
        grid_spec=grid_spec,
        out_shape=out_shape,
        compiler_params=_cparams(),
        name="moe_combine",
    )(gdest, rt, lstart, x1, p_prompt, p_sample, ple_gate.astype(BF16), ple_proj.astype(BF16),
      ple_norm.reshape(1, D_MODEL), final_norm.reshape(1, D_MODEL), y)
    return outs if final else outs[0]


def _layer(cfg, layer, final, x, p_prompt, p_sample, tabs, norm1, w_in, conv_w, dn_a_log,
           dn_dt_bias, dn_norm, ret_decay, ret_norm, w_out, norm2, router_w, router_b, w_gate,
           b_gate, w_up, b_up, w_down, b_down, ple_proj, ple_gate, ple_norm, final_norm):
    proj = _in_proj(cfg, x, norm1[layer], _reorder_w_in(w_in[layer]), tabs[0], tabs[1], 512)
    qkv, gates = _dn_prep(cfg, proj, conv_w[layer], dn_a_log[layer], dn_dt_bias[layer], 512)
    dn_f, dn_b = _dn_scan(cfg, qkv, gates, 256)
    ret_f, ret_b = _ret_scan(cfg, proj, _ret_tables(ret_decay[layer]), 256)
    att = _attention(cfg, proj)
    x1, h2, rt, cnt = _post(cfg, dn_f, dn_b, ret_f, ret_b, att, proj, x, w_out[layer],
                            dn_norm[layer], ret_norm[layer], norm2[layer], router_w[layer],
                            router_b[layer])
    n_rows = _moe_rows(cfg)
    lstart, gdest, block_exp, last_blk, nvalid = _route_tables(cnt[:, 0, :], n_rows)
    xin = _dispatch(cfg, h2, rt, lstart, gdest, last_blk, nvalid, n_rows)
    y = _experts(layer, xin, block_exp, nvalid, w_gate, b_gate, w_up, b_up, w_down, b_down)
    return _combine(cfg, layer, final, rt, lstart, gdest, x1, p_prompt, p_sample, ple_gate[layer],
                    ple_proj[layer], ple_norm[layer], final_norm, y)


def _trunk(cfg, x_parts, p_prompt, p_sample, final_norm, params):
    tabs = _rope_tables(cfg.smax)
    depth = p_prompt.shape[0]
    x = x_parts
    for layer in range(depth):
        final = layer == depth - 1
        out = _layer(cfg, layer, final, x, p_prompt, p_sample, tabs, *params, final_norm)
        x = out if final else (out,)
    return x


def kernel(x_prompt, x_sample, p_prompt, p_sample, norm1, w_in, conv_w, dn_a_log, dn_dt_bias,
           dn_norm, ret_decay, ret_norm, w_out, norm2, router_w, router_b, w_gate, b_gate,
           w_up, b_up, w_down, b_down, ple_proj, ple_gate, ple_norm, final_norm):
    b0, s0, d = x_prompt.shape
    b1, s1, _ = x_sample.shape
    cfg = Cfg(b0, s0, b1, s1)
    depth = p_prompt.shape[0]
    params = (norm1, w_in, conv_w, dn_a_log, dn_dt_bias, dn_norm, ret_decay, ret_norm, w_out,
              norm2, router_w, router_b, w_gate, b_gate, w_up, b_up, w_down, b_down,
              ple_proj, ple_gate, ple_norm)
    y0, y1 = _trunk(cfg, (x_prompt.reshape(cfg.t0, d), x_sample.reshape(cfg.t1, d)),
                    p_prompt.reshape(depth, cfg.t0, PLE_DIM), p_sample.reshape(depth, cfg.t1, PLE_DIM),
                    final_norm, params)
    return (y0.reshape(b0, s0, d), y1.reshape(b1, s1, d))
```

```python
import functools
import math

import jax
import jax.numpy as jnp
import numpy as np
from jax import lax
from jax.experimental import pallas as pl
from jax.experimental.pallas import tpu as pltpu

F32 = jnp.float32
BF16 = jnp.bfloat16
I32 = jnp.int32

D_MODEL = 1024
HEAD_DIM = 64
DN_HEADS = 4
RET_HEADS = 4
ATT_HEADS = 8
DN_WIDTH = 256
RET_WIDTH = 256
ATT_WIDTH = 512
CONV_W = 4
RET_THETA = 10000.0
ROPE_THETA = 500000.0
ROPE_DIM = HEAD_DIM // 4
DILATED_PATTERNS = ((128, 1), (512, 4), (2048, 16))
ATT_HALF = 64
N_EXPERTS = 32
TOP_K = 4
SWIGLU_ALPHA = 1.702
SWIGLU_LIMIT = 7.0
PLE_DIM = 256
NORM_EPS = 1e-6
NEG_BIG = -1e30

LANES = 128
SUBLANES = 8
VMEM_LIMIT = 56 * 1024 * 1024

COL_DN_QKV = 0
COL_DN_Z = 768
COL_RQ = 1024
COL_RK = 1280
COL_RV = 1536
COL_RG = 1792
COL_AQ = 2048
COL_AK = 2560
COL_AV = 3072
COL_AB = 3584
PROJ_W = 3840

PROJ_TM = 512
SCAN_TM = 256
CHUNK = 64
CAT = 4 * CHUNK


class Cfg:
    def __init__(self, b0, s0, b1, s1):
        self.b0, self.s0, self.b1, self.s1 = b0, s0, b1, s1
        self.t0 = b0 * s0
        self.t1 = b1 * s1
        self.t = self.t0 + self.t1
        self.smax = max(s0, s1)

    def pos(self, row):
        return jnp.where(row < self.t0, row % self.s0, (row - self.t0) % self.s1)

    def seq_len(self, row):
        return jnp.where(row < self.t0, self.s0, self.s1)


def _cparams(n_axes=1, sem=None):
    return pltpu.CompilerParams(
        dimension_semantics=tuple(sem or ("arbitrary",) * n_axes),
        vmem_limit_bytes=VMEM_LIMIT)


def _group_specs(cfg, tm, width, n_parts, lead=None):
    pre_b = () if lead is None else (None,)
    pre_i = () if lead is None else (lead,)
    if n_parts == 1:
        return [pl.BlockSpec(pre_b + (tm, width), lambda i, *_: pre_i + (i, 0))]
    n0 = cfg.t0 // tm
    return [pl.BlockSpec(pre_b + (tm, width), lambda i, *_: pre_i + (jnp.minimum(i, n0 - 1), 0)),
            pl.BlockSpec(pre_b + (tm, width), lambda i, *_: pre_i + (jnp.maximum(i - n0, 0), 0))]


def _group_load(cfg, tm, refs, rows=slice(None)):
    if len(refs) == 1:
        return refs[0][rows, :]
    return jnp.where(pl.program_id(0) < cfg.t0 // tm, refs[0][rows, :], refs[1][rows, :])


def _split_bf16(x):
    hi = x.astype(BF16)
    lo = (x - hi.astype(F32)).astype(BF16)
    return hi, lo


def _dot(a, b):
    return jnp.dot(a, b, preferred_element_type=F32)


def _dot_x2(a, b_exact):
    hi, lo = _split_bf16(a)
    return _dot(hi, b_exact) + _dot(lo, b_exact)


def _dot_lx2(a_exact, b):
    hi, lo = _split_bf16(b)
    return _dot(a_exact, hi) + _dot(a_exact, lo)


def _dot_x3(a, b):
    ah, al = _split_bf16(a)
    bh, bl = _split_bf16(b)
    return _dot(ah, bh) + (_dot(al, bh) + _dot(ah, bl))


def _sigmoid(x):
    return 0.5 * jnp.tanh(0.5 * x) + 0.5


def _silu(x):
    return x * _sigmoid(x)


def _head_sumsq(x, hsum):
    return _dot((x * x).astype(BF16), hsum)


def _head_sum_matrix(width):
    r = lax.broadcasted_iota(I32, (width, width), 0) // HEAD_DIM
    c = lax.broadcasted_iota(I32, (width, width), 1) // HEAD_DIM
    return jnp.where(r == c, 1.0, 0.0).astype(BF16)


_SECTIONS = tuple((c, 1 if c in (COL_RQ, COL_RK) else 2 if COL_AQ <= c < COL_AV else 0)
                  for c in range(0, PROJ_W, 256))


def _rope_tables(smax):
    pos = jnp.arange(smax, dtype=F32)[:, None]
    d = jnp.arange(256) % HEAD_DIM

    def build(inv, half):
        rot = 2 * half
        ang = pos * inv[d % half][None, :]
        cos, sin = jnp.cos(ang), jnp.sin(ang)
        in_rot = (d < rot)[None, :]
        first = (d < half)[None, :]
        c = jnp.where(in_rot, cos, 1.0)
        s_plus = jnp.where(in_rot & ~first, sin, 0.0)
        s_minus = jnp.where(first, -sin, 0.0)
        return jnp.stack([c, s_plus, s_minus])

    inv_ret = 1.0 / jnp.power(jnp.float32(RET_THETA), jnp.linspace(0.0, 1.0, HEAD_DIM // 2, dtype=F32))
    inv_att = jnp.power(jnp.float32(ROPE_THETA), -jnp.arange(0, ROPE_DIM, 2, dtype=F32) / ROPE_DIM)
    return build(inv_ret, HEAD_DIM // 2), build(inv_att, ROPE_DIM // 2)


def _in_proj_kernel(cfg, tm, nx, *refs):
    g_ref, w_ref, tr_ref, ta_ref, o_ref = refs[nx:]
    x = _group_load(cfg, tm, refs[:nx])
    y = x * lax.rsqrt(jnp.mean(x * x, axis=-1, keepdims=True) + NORM_EPS)
    h = (y * g_ref[...]).astype(BF16)
    for col, kind in _SECTIONS:
        acc = _dot(h, w_ref[:, col:col + 256])
        if kind:
            t_ref, half = (tr_ref, HEAD_DIM // 2) if kind == 1 else (ta_ref, ROPE_DIM // 2)
            acc = (acc * t_ref[0] + pltpu.roll(acc, half, 1) * t_ref[1]
                   + pltpu.roll(acc, 256 - half, 1) * t_ref[2])
        o_ref[:, col:col + 256] = acc


def _in_proj(cfg, xs, g1, w_bf16, tab_ret, tab_att, tm):
    n = cfg.t // tm

    def tab_map(i):
        return (0, cfg.pos(i * tm) // tm, 0)

    return pl.pallas_call(
        functools.partial(_in_proj_kernel, cfg, tm, len(xs)),
        grid=(n,),
        in_specs=_group_specs(cfg, tm, D_MODEL, len(xs)) + [
            pl.BlockSpec((1, D_MODEL), lambda i: (0, 0)),
            pl.BlockSpec((D_MODEL, PROJ_W), lambda i: (0, 0)),
            pl.BlockSpec((3, tm, 256), tab_map),
            pl.BlockSpec((3, tm, 256), tab_map),
        ],
        out_specs=pl.BlockSpec((tm, PROJ_W), lambda i: (i, 0)),
        out_shape=jax.ShapeDtypeStruct((cfg.t, PROJ_W), F32),
        compiler_params=_cparams(),
        name="in_proj",
    )(*xs, g1.reshape(1, D_MODEL), w_bf16, tab_ret, tab_att)


def _reorder_w_in(w_in):
    ab = w_in[:, 1024:1040]
    rest = w_in[:, 1040:]
    pad = jnp.zeros((w_in.shape[0], PROJ_W - 3600), w_in.dtype)
    return jnp.concatenate([w_in[:, :1024], rest, ab, pad], axis=1).astype(BF16)


def _dn_prep_kernel(cfg, tm, cur_ref, prev_ref, next_ref, ab_ref, cw_ref, ga_ref, gb_ref,
                    qkv_ref, gate_ref, ext_ref):
    i = pl.program_id(0)
    pos = cfg.pos(i * tm)
    first = pos == 0
    last = pos + tm == cfg.seq_len(i * tm)
    ext_ref[0:8, :] = jnp.where(first, 0.0, prev_ref[...])
    ext_ref[8:8 + tm, :] = cur_ref[...]
    ext_ref[8 + tm:16 + tm, :] = jnp.where(last, 0.0, next_ref[...])
    y = cw_ref[0:1, :] * ext_ref[6:6 + tm, :]
    for j in range(1, CONV_W):
        y = y + cw_ref[j:j + 1, :] * ext_ref[6 + j:6 + j + tm, :]
    y = _silu(y)
    hsum = _head_sum_matrix(DN_WIDTH)
    q = y[:, 0:DN_WIDTH]
    k = y[:, DN_WIDTH:2 * DN_WIDTH]
    qkv_ref[:, 0:DN_WIDTH] = q * lax.rsqrt(_head_sumsq(q, hsum) + NORM_EPS) * (HEAD_DIM ** -0.5)
    qkv_ref[:, DN_WIDTH:2 * DN_WIDTH] = k * lax.rsqrt(_head_sumsq(k, hsum) + NORM_EPS)
    qkv_ref[:, 2 * DN_WIDTH:] = y[:, 2 * DN_WIDTH:]
    ab = ab_ref[:, 0:LANES]
    xa = ab + gb_ref[...]
    softplus = jnp.maximum(xa, 0.0) + jnp.log(1.0 + jnp.exp(-jnp.abs(xa)))
    lane = lax.broadcasted_iota(I32, ab.shape, 1)
    gate_ref[...] = jnp.where(lane < 2 * DN_HEADS, ga_ref[...] * softplus, _sigmoid(ab))


def _dn_prep(cfg, proj, conv_w, a_log, dt_bias, tm):
    n = cfg.t // tm
    w3 = 3 * DN_WIDTH
    nh = tm // 8
    last8 = cfg.t // 8 - 1
    neg_a = jnp.zeros((1, LANES), F32).at[0, :2 * DN_HEADS].set(-jnp.exp(a_log.reshape(-1)))
    dtb = jnp.zeros((1, LANES), F32).at[0, :2 * DN_HEADS].set(dt_bias.reshape(-1))
    return pl.pallas_call(
        functools.partial(_dn_prep_kernel, cfg, tm),
        grid=(n,),
        in_specs=[
            pl.BlockSpec((tm, w3), lambda i: (i, 0)),
            pl.BlockSpec((8, w3), lambda i: (jnp.maximum(i * nh - 1, 0), 0)),
            pl.BlockSpec((8, w3), lambda i: (jnp.minimum((i + 1) * nh, last8), 0)),
            pl.BlockSpec((tm, 256), lambda i: (i, COL_AB // 256)),
            pl.BlockSpec((CONV_W, w3), lambda i: (0, 0)),
            pl.BlockSpec((1, LANES), lambda i: (0, 0)),
            pl.BlockSpec((1, LANES), lambda i: (0, 0)),
        ],
        out_specs=[
            pl.BlockSpec((tm, w3), lambda i: (i, 0)),
            pl.BlockSpec((tm, LANES), lambda i: (i, 0)),
        ],
        out_shape=[jax.ShapeDtypeStruct((cfg.t, w3), F32),
                   jax.ShapeDtypeStruct((cfg.t, LANES), F32)],
        scratch_shapes=[pltpu.VMEM((tm + 16, w3), F32)],
        compiler_params=_cparams(),
        name="dn_prep",
    )(proj, proj, proj, proj, conv_w, neg_a, dtb)


def _bd_mask():
    r = lax.broadcasted_iota(I32, (CAT, CAT), 0) // CHUNK
    c = lax.broadcasted_iota(I32, (CAT, CAT), 1) // CHUNK
    return r == c, jnp.where(r == c, 1.0, 0.0).astype(BF16)


def _tile4(x):
    xb = x.astype(BF16)
    return jnp.concatenate([xb, xb, xb, xb], axis=0)


def _bd(x, bdmask):
    return _tile4(x) * bdmask


def _dot_t0(a, b):
    return lax.dot_general(a, b, (((0,), (0,)), ((), ())), preferred_element_type=F32)


def _dot_t1(a, b):
    return lax.dot_general(a, b, (((1,), (1,)), ((), ())), preferred_element_type=F32)


def _chunk_ij(rows):
    i = lax.broadcasted_iota(I32, (rows, CAT), 0) % CHUNK
    j = lax.broadcasted_iota(I32, (rows, CAT), 1) % CHUNK
    return i, j


def _segments(cfg, tm, want):
    for nseg in (want, 2, 1):
        seg = cfg.t // nseg
        if cfg.t % nseg == 0 and seg % tm == 0 and all(
                (s * seg) % cfg.s0 == 0 if s * seg <= cfg.t0 else (s * seg - cfg.t0) % cfg.s1 == 0
                for s in range(nseg)):
            return nseg
    return 1


def _stream_rows(nseg, per, tm):
    fns = []
    for s in range(nseg):
        fns.append(lambda i, s=s: (s * per + i) * tm)
        fns.append(lambda i, s=s: (s * per + per - 1 - i) * tm)
    return fns


def _stream_views(nseg, fwd_ref, bwd_ref):
    out = []
    for s in range(nseg):
        out += [fwd_ref.at[s], bwd_ref.at[s]]
    return out


def _dn_streams(revs, qkv_refs, gate_refs, o_refs, s_refs, tm):
    nc = tm // CHUNK
    bd_bool, bdmask = _bd_mask()
    i, j = _chunk_ij(tm)
    ic, jc = _chunk_ij(CHUNK)
    eye = jnp.where(ic == jc, 1.0, 0.0)
    lvl_masks = {lvl: _bd(jnp.where(((ic >> lvl) == (jc >> lvl)) & ((ic >> (lvl - 1)) != (jc >> (lvl - 1))),
                                    1.0, 0.0), bdmask) for lvl in range(2, 7)}
    l = lax.broadcasted_iota(I32, (LANES, 2 * CAT), 0)
    c = lax.broadcasted_iota(I32, (LANES, 2 * CAT), 1)
    ti = lax.broadcasted_iota(I32, (tm, tm), 0)
    tk = lax.broadcasted_iota(I32, (tm, tm), 1)
    same = (ti // CHUNK) == (tk // CHUNK)
    consts = {}
    for rev in sorted(set(revs)):
        base = (4 if rev else 0) + jnp.where(c >= CAT, 2 * DN_HEADS, 0)
        consts[rev] = dict(
            sel=jnp.where(l == base + (c % CAT) // CHUNK, 1.0, 0.0).astype(BF16),
            tri=jnp.where(same & ((tk >= ti) if rev else (tk <= ti)), 1.0, 0.0).astype(BF16),
            between=(i < j) if rev else (i > j),
            incl=(i <= j) if rev else (i >= j),
            strict=(ic < jc) if rev else (ic > jc))

    st = []
    for rev, qkv_ref, gate_ref in zip(revs, qkv_refs, gate_refs):
        cs = consts[rev]
        gb = _dot_x2(gate_ref[...], cs["sel"])
        g_cat, beta = gb[:, :CAT], gb[:, CAT:]
        gc = _dot_lx2(cs["tri"], g_cat)
        diff = _dot_lx2(cs["tri"], jnp.where(cs["between"], g_cat, 0.0))
        st.append(dict(rev=rev, beta=beta, gc=gc, egc=jnp.exp(gc),
                       decay=jnp.where(cs["incl"], jnp.exp(diff), 0.0),
                       q=qkv_ref[:, 0:CAT], k=qkv_ref[:, CAT:2 * CAT], v=qkv_ref[:, 2 * CAT:3 * CAT]))

    chains = [(si, ci) for si in range(len(st)) for ci in range(nc)]
    attn, amat, tinv = {}, {}, {}
    for si, ci in chains:
        s_, r = st[si], slice(ci * CHUNK, (ci + 1) * CHUNK)
        qk_kk = _dot_t1(jnp.concatenate([s_["q"][r], s_["k"][r]], axis=0).astype(BF16),
                        _bd(s_["k"][r], bdmask))
        attn[si, ci] = qk_kk[:CHUNK] * s_["decay"][r]
        a = jnp.where(consts[s_["rev"]]["strict"], qk_kk[CHUNK:] * s_["beta"][r] * s_["decay"][r], 0.0)
        amat[si, ci] = _tile4(a)
        tinv[si, ci] = eye - jnp.where((ic >> 1) == (jc >> 1), a, 0.0)
    for lvl in range(2, 7):
        xs = {key: _dot(tinv[key].astype(BF16), amat[key] * lvl_masks[lvl]) for key in chains}
        for key in chains:
            tinv[key] = tinv[key] - _dot(xs[key].astype(BF16), _bd(tinv[key], bdmask))

    states = [s_ref[...] for s_ref in s_refs]
    for step in range(nc):
        idx = [(nc - 1 - step) if s_["rev"] else step for s_ in st]
        rs = [slice(ci * CHUNK, (ci + 1) * CHUNK) for ci in idx]
        x1s = [_dot(jnp.concatenate([s_["k"][r] * s_["beta"][r] * s_["egc"][r], s_["q"][r] * s_["egc"][r]],
                                    axis=0).astype(BF16), state.astype(BF16))
               for s_, r, state in zip(st, rs, states)]
        vns = [_dot(tinv[si, ci].astype(BF16), _bd(s_["v"][r] * s_["beta"][r] - x1[:CHUNK], bdmask))
               for si, (s_, r, ci, x1) in enumerate(zip(st, rs, idx, x1s))]
        for si, (s_, r, ci, x1, vn, o_ref) in enumerate(zip(st, rs, idx, x1s, vns, o_refs)):
            o_ref[r, :] = x1[CHUNK:] + _dot(attn[si, ci].astype(BF16), _bd(vn, bdmask))
            gcc = s_["gc"][r]
            g_last = gcc[0:1] if s_["rev"] else gcc[CHUNK - 1:CHUNK]
            k_dec = s_["k"][r] * jnp.exp(g_last - gcc)
            states[si] = states[si] * jnp.exp(g_last) + jnp.where(
                bd_bool, _dot_t0(k_dec.astype(BF16), vn.astype(BF16)), 0.0)
    for s_ref, state in zip(s_refs, states):
        s_ref[...] = state


def _reset_states(cfg, tm, row_fns, s_refs):
    i = pl.program_id(0)
    for si, (row_fn, s_ref) in enumerate(zip(row_fns, s_refs)):
        row = row_fn(i)
        edge = (cfg.pos(row) + tm == cfg.seq_len(row)) if si % 2 else (cfg.pos(row) == 0)

        @pl.when(edge)
        def _(s_ref=s_ref):
            s_ref[...] = jnp.zeros_like(s_ref)


def _dn_scan_kernel(cfg, tm, nseg, per, qf_ref, gf_ref, qb_ref, gb_ref, of_ref, ob_ref, *s_refs):
    _reset_states(cfg, tm, _stream_rows(nseg, per, tm), s_refs)
    _dn_streams([si % 2 == 1 for si in range(2 * nseg)], _stream_views(nseg, qf_ref, qb_ref),
                _stream_views(nseg, gf_ref, gb_ref), _stream_views(nseg, of_ref, ob_ref), s_refs, tm)


def _scan_specs(nseg, per, tm, width, col_block=0):
    return (pl.BlockSpec((nseg, tm, width), lambda i: (0, i, col_block)),
            pl.BlockSpec((nseg, tm, width), lambda i: (0, per - 1 - i, col_block)))


def _dn_scan(cfg, qkv, gate, tm, want_segments=2):
    nseg = _segments(cfg, tm, want_segments)
    per = cfg.t // nseg // tm
    w3 = 3 * DN_WIDTH
    qf, qb = _scan_specs(nseg, per, tm, w3)
    gf, gb = _scan_specs(nseg, per, tm, LANES)
    of, ob = _scan_specs(nseg, per, tm, DN_WIDTH)
    qkv3 = qkv.reshape(nseg, cfg.t // nseg, w3)
    gate3 = gate.reshape(nseg, cfg.t // nseg, LANES)
    o_f, o_b = pl.pallas_call(
        functools.partial(_dn_scan_kernel, cfg, tm, nseg, per),
        grid=(per,),
        in_specs=[qf, gf, qb, gb],
        out_specs=[of, ob],
        out_shape=[jax.ShapeDtypeStruct((nseg, cfg.t // nseg, DN_WIDTH), F32)] * 2,
        scratch_shapes=[pltpu.VMEM((CAT, CAT), F32)] * (2 * nseg),
        compiler_params=_cparams(),
        name="dn_scan",
    )(qkv3, gate3, qkv3, gate3)
    return o_f.reshape(cfg.t, DN_WIDTH), o_b.reshape(cfg.t, DN_WIDTH)


def _ret_tables(decay_logit):
    lg = jnp.log(jax.nn.sigmoid(decay_logit.astype(F32)))
    i = jnp.arange(CHUNK, dtype=F32)[:, None]
    j = (jnp.arange(CAT) % CHUNK).astype(F32)[None, :]
    scale = HEAD_DIM ** -0.5
    out = []
    for d in range(2):
        lgl = jnp.repeat(lg[d], CHUNK)[None, :]
        rel = (j - i) if d else (i - j)
        dmat = jnp.where(rel >= 0, jnp.exp(jnp.where(rel >= 0, rel, 0.0) * lgl), 0.0) * scale
        qdec = jnp.exp(((CHUNK - i) if d else (i + 1.0)) * lgl)
        kdec = jnp.exp((i if d else (CHUNK - 1.0 - i)) * lgl) * scale
        cd = jnp.broadcast_to(jnp.exp(CHUNK * lgl), (CHUNK, CAT))
        out.append(jnp.stack([dmat, qdec, kdec, cd]))
    return jnp.stack(out)


def _ret_streams(revs, q_refs, k_refs, v_refs, tab_ref, o_refs, s_refs, tm):
    nc = tm // CHUNK
    bd_bool, bdmask = _bd_mask()
    tabs = {rev: [tab_ref[1 if rev else 0, t] for t in range(4)] for rev in sorted(set(revs))}
    inner = {}
    for si, rev in enumerate(revs):
        dmat = tabs[rev][0]
        for ci in range(nc):
            r = slice(ci * CHUNK, (ci + 1) * CHUNK)
            scores = _dot_t1(q_refs[si][r, :].astype(BF16), _bd(k_refs[si][r, :], bdmask)) * dmat
            inner[si, ci] = _dot(scores.astype(BF16), _bd(v_refs[si][r, :], bdmask))
    states = [s_ref[...] for s_ref in s_refs]
    for step in range(nc):
        for si, rev in enumerate(revs):
            _, qdec, kdec, cd = tabs[rev]
            ci = (nc - 1 - step) if rev else step
            r = slice(ci * CHUNK, (ci + 1) * CHUNK)
            qc, kc, vc = q_refs[si][r, :], k_refs[si][r, :], v_refs[si][r, :]
            o_refs[si][r, :] = inner[si, ci] + _dot((qc * qdec).astype(BF16), states[si].astype(BF16))
            kv = _dot_t0((kc * kdec).astype(BF16), vc.astype(BF16))
            states[si] = states[si] * cd[0:1] + jnp.where(bd_bool, kv, 0.0)
    for s_ref, state in zip(s_refs, states):
        s_ref[...] = state


def _ret_scan_kernel(cfg, tm, nseg, per, qf, kf, vf, qb, kb, vb, tab_ref, of_ref, ob_ref, *s_refs):
    _reset_states(cfg, tm, _stream_rows(nseg, per, tm), s_refs)
    _ret_streams([si % 2 == 1 for si in range(2 * nseg)], _stream_views(nseg, qf, qb),
                 _stream_views(nseg, kf, kb), _stream_views(nseg, vf, vb), tab_ref,
                 _stream_views(nseg, of_ref, ob_ref), s_refs, tm)


def _ret_scan(cfg, proj, tables, tm, want_segments=2):
    nseg = _segments(cfg, tm, want_segments)
    per = cfg.t // nseg // tm
    proj3 = proj.reshape(nseg, cfg.t // nseg, PROJ_W)
    specs = [_scan_specs(nseg, per, tm, 256, col // 256) for col in (COL_RQ, COL_RK, COL_RV)]
    of, ob = _scan_specs(nseg, per, tm, RET_WIDTH)
    o_f, o_b = pl.pallas_call(
        functools.partial(_ret_scan_kernel, cfg, tm, nseg, per),
        grid=(per,),
        in_specs=[s[0] for s in specs] + [s[1] for s in specs]
                 + [pl.BlockSpec((2, 4, CHUNK, CAT), lambda i: (0, 0, 0, 0))],
        out_specs=[of, ob],
        out_shape=[jax.ShapeDtypeStruct((nseg, cfg.t // nseg, RET_WIDTH), F32)] * 2,
        scratch_shapes=[pltpu.VMEM((CAT, CAT), F32)] * (2 * nseg),
        compiler_params=_cparams(),
        name="ret_scan",
    )(proj3, proj3, proj3, proj3, proj3, proj3, tables)
    return o_f.reshape(cfg.t, RET_WIDTH), o_b.reshape(cfg.t, RET_WIDTH)


ATT_TQ = 2048
ATT_HALO = 1024
ATT_QB = 128


def _ds(start, size, stride):
    return pl.ds(start, size) if stride == 1 else pl.ds(start, size, stride=stride)


ATT_NB = 4


def _att_kernel(cfg, q_ref, kp_ref, kc_ref, kn_ref, vp_ref, vc_ref, vn_ref, o_ref,
                kbuf, vbuf, bias_sc, m_sc, l_sc, acc_sc):
    tq, halo, qb = ATT_TQ, ATT_HALO, ATT_QB
    kw = qb + 2 * ATT_HALF
    i = pl.program_id(1)
    pos0 = cfg.pos(i * tq)
    slen = cfg.seq_len(i * tq)
    kbuf[0:halo, :] = kp_ref[...]
    kbuf[halo:halo + tq, :] = kc_ref[...]
    kbuf[halo + tq:, :] = kn_ref[...]
    vbuf[0:halo, :] = vp_ref[...]
    vbuf[halo:halo + tq, :] = vc_ref[...]
    vbuf[halo + tq:, :] = vn_ref[...]
    head0 = lax.broadcasted_iota(I32, (qb, LANES), 1) < HEAD_DIM
    qi = lax.broadcasted_iota(I32, (qb, kw), 0)
    kj = lax.broadcasted_iota(I32, (qb, kw), 1)
    bias_sc[...] = jnp.where((kj - qi >= 0) & (kj - qi <= 2 * ATT_HALF), 0.0, NEG_BIG)
    krow = lax.broadcasted_iota(I32, (1, kw), 1)
    qscale = HEAD_DIM ** -0.5 * math.log2(math.e)
    ones_v = jnp.ones((kw, LANES), BF16)

    for p, (_, dil) in enumerate(DILATED_PATTERNS):
        def body(it, carry, p=p, dil=dil):
            where_q, where_k, scores = [], [], []
            for jj in range(ATT_NB):
                u = it * ATT_NB + jj
                r = u % dil
                b = u // dil
                qs = r + dil * qb * b
                ks = halo + r + dil * (qb * b - ATT_HALF)
                where_q.append(qs)
                where_k.append(ks)
                q = q_ref[_ds(qs, qb, dil), :] * qscale
                k = kbuf[_ds(ks, kw, dil), :].astype(BF16)
                kpos = pos0 // dil + qb * b - ATT_HALF + krow
                valid = jnp.where((kpos >= 0) & (kpos < slen // dil), 0.0, NEG_BIG)
                for h in range(2):
                    qh = jnp.where(head0 if h == 0 else ~head0, q, 0.0).astype(BF16)
                    scores.append(_dot_t1(qh, k) + bias_sc[...] + valid)
            ms = [jnp.max(s, axis=-1, keepdims=True) for s in scores]
            ps = [jnp.exp2(s - m).astype(BF16) for s, m in zip(scores, ms)]
            for jj in range(ATT_NB):
                v = vbuf[_ds(where_k[jj], kw, dil), :].astype(BF16)
                vext = jnp.concatenate([v, ones_v], axis=1)
                a0 = _dot(ps[2 * jj], vext)
                a1 = _dot(ps[2 * jj + 1], vext)
                rows = _ds(where_q[jj], qb, dil)
                m_sc[p, rows, :] = jnp.where(head0, ms[2 * jj], ms[2 * jj + 1])
                l_sc[p, rows, :] = jnp.where(head0, a0[:, LANES:], a1[:, LANES:])
                acc_sc[p, rows, :] = jnp.where(head0, a0[:, :LANES], a1[:, :LANES])
            return carry

        lax.fori_loop(0, tq // qb // ATT_NB, body, 0)

    m_all = [m_sc[p] for p in range(3)]
    mx = jnp.maximum(jnp.maximum(m_all[0], m_all[1]), m_all[2])
    num = jnp.zeros((tq, LANES), F32)
    den = jnp.zeros((tq, LANES), F32)
    for p in range(3):
        w = jnp.exp2(m_all[p] - mx)
        num = num + w * acc_sc[p]
        den = den + w * l_sc[p]
    o_ref[...] = num / den


def _attention(cfg, proj):
    tq, halo = ATT_TQ, ATT_HALO
    n = cfg.t // tq
    per = tq // halo
    nh = cfg.t // halo

    def cur(col):
        return pl.BlockSpec((tq, LANES), lambda hp, i: (i, col // LANES + hp))

    def prev(col):
        return pl.BlockSpec((halo, LANES), lambda hp, i: (jnp.maximum(i * per - 1, 0), col // LANES + hp))

    def nxt(col):
        return pl.BlockSpec((halo, LANES), lambda hp, i: (jnp.minimum((i + 1) * per, nh - 1), col // LANES + hp))

    return pl.pallas_call(
        functools.partial(_att_kernel, cfg),
        grid=(ATT_HEADS // 2, n),
        in_specs=[cur(COL_AQ), prev(COL_AK), cur(COL_AK), nxt(COL_AK),
                  prev(COL_AV), cur(COL_AV), nxt(COL_AV)],
        out_specs=pl.BlockSpec((tq, LANES), lambda hp, i: (i, hp)),
        out_shape=jax.ShapeDtypeStruct((cfg.t, ATT_WIDTH), F32),
        scratch_shapes=[pltpu.VMEM((tq + 2 * halo, LANES), F32)] * 2
                       + [pltpu.VMEM((ATT_QB, ATT_QB + 2 * ATT_HALF), F32)]
                       + [pltpu.VMEM((3, tq, LANES), F32)] * 3,
        compiler_params=_cparams(2),
        name="attention",
    )(proj, proj, proj, proj, proj, proj, proj)


MOE_TM = 512
MOE_R = 4 * MOE_TM + 256
MOE_NG = MOE_R // SUBLANES
MOE_RC = MOE_R // 3
MOE_XB = 512
MOE_DUMP = -(-MOE_R // MOE_XB) * MOE_XB
RT_ID, RT_RANK, RT_GATE = 0, 4, 8
POST_SUB = MOE_TM
DMA_UNROLL = 8


def _rms(x, g):
    return x * lax.rsqrt(jnp.mean(x * x, axis=-1, keepdims=True) + NORM_EPS) * g


def _post_kernel(cfg, nx, dnf, dnb, z_ref, rtf, rtb, rg_ref, att_ref, *refs):
    (wout_ref, dnn_ref, rtn_ref, n2_ref, rw_ref, rb_ref, ltri_ref,
     x1_ref, h2_ref, rt_ref, cnt_ref) = refs[nx:]
    hsum = _head_sum_matrix(DN_WIDTH)
    inv = 1.0 / HEAD_DIM
    tm, sub = MOE_TM, POST_SUB
    lane = lax.broadcasted_iota(I32, (sub, LANES), 1)
    lane_f = lane.astype(F32)
    parts = []
    for s in range(tm // sub):
        r = slice(s * sub, (s + 1) * sub)
        o = dnf[r, :] + dnb[r, :]
        dn = o * lax.rsqrt(_head_sumsq(o, hsum) * inv + NORM_EPS) * dnn_ref[...] * _silu(z_ref[r, :])
        o = rtf[r, :] + rtb[r, :]
        oc = o - _dot_x2(o, hsum) * inv
        ret = (oc * lax.rsqrt(_head_sumsq(oc, hsum) * inv + NORM_EPS) * rtn_ref[...]
               * _silu(rg_ref[r, :]))
        mixed = (_dot(dn.astype(BF16), wout_ref[0:256, :]) + _dot(ret.astype(BF16), wout_ref[256:512, :])
                 + _dot(att_ref[r, :].astype(BF16), wout_ref[512:1024, :]))
        x1 = _group_load(cfg, MOE_TM, refs[:nx], r) + mixed
        x1_ref[r, :] = x1
        h2 = _rms(x1, n2_ref[...])
        h2_ref[r, :] = h2.astype(BF16)
        hh, hl = _split_bf16(h2)
        work = _dot(hh, rw_ref[0]) + (_dot(hl, rw_ref[0]) + _dot(hh, rw_ref[1])) + rb_ref[...]
        vals, ids = [], []
        for _ in range(TOP_K):
            m = jnp.max(work, axis=-1, keepdims=True)
            idx = jnp.min(jnp.where(work == m, lane_f, float(LANES)), axis=-1, keepdims=True).astype(I32)
            vals.append(m)
            ids.append(idx)
            work = jnp.where(lane == idx, -3e38, work)
        es = [jnp.exp(v - vals[0]) for v in vals]
        onehots = [jnp.where(lane == idx, 1.0, 0.0) for idx in ids]
        parts.append(dict(ids=ids, es=es, tot=es[0] + es[1] + es[2] + es[3], onehots=onehots,
                          msum=onehots[0] + onehots[1] + onehots[2] + onehots[3]))
    msum = jnp.concatenate([pt["msum"] for pt in parts], axis=0)
    before = _dot(ltri_ref[...], msum.astype(BF16))
    for s, pt in enumerate(parts):
        r = slice(s * sub, (s + 1) * sub)
        rec = jnp.zeros((sub, LANES), I32)
        for k in range(TOP_K):
            rank = jnp.sum(pt["onehots"][k] * before[r], axis=-1, keepdims=True).astype(I32)
            gate_bits = lax.bitcast_convert_type(pt["es"][k] / pt["tot"], I32)
            rec = jnp.where(lane == RT_ID + k, pt["ids"][k], rec)
            rec = jnp.where(lane == RT_RANK + k, rank, rec)
            rec = jnp.where(lane == RT_GATE + k, gate_bits, rec)
        rt_ref[r, :] = rec
    cnt_ref[0] = (before[tm - 1:tm, :] + msum[tm - 1:tm, :]).astype(I32)


def _post(cfg, dn_f, dn_b, ret_f, ret_b, att, proj, xs, w_out, dn_norm, ret_norm, norm2,
          router_w, router_b):
    tm = MOE_TM
    n = cfg.t // tm
    rw = jnp.zeros((D_MODEL, LANES), F32).at[:, :N_EXPERTS].set(router_w)
    rw_hi = rw.astype(BF16)
    rw2 = jnp.stack([rw_hi, (rw - rw_hi.astype(F32)).astype(BF16)])
    rb = jnp.full((1, LANES), NEG_BIG, F32).at[0, :N_EXPERTS].set(router_b)
    ltri = (jnp.arange(tm)[:, None] > jnp.arange(tm)[None, :]).astype(BF16)
    row = lambda w: pl.BlockSpec((tm, w), lambda i: (i, 0))
    col = lambda c: pl.BlockSpec((tm, 256), lambda i: (i, c // 256))
    const = lambda *s: pl.BlockSpec(s, lambda i: (0,) * len(s))
    return pl.pallas_call(
        functools.partial(_post_kernel, cfg, len(xs)),
        grid=(n,),
        in_specs=[row(256), row(256), col(COL_DN_Z), row(256), row(256), col(COL_RG), row(512)]
                 + _group_specs(cfg, tm, D_MODEL, len(xs))
                 + [const(D_MODEL, D_MODEL), const(1, 256), const(1, 256),
                    const(1, D_MODEL), const(2, D_MODEL, LANES), const(1, LANES), const(tm, tm)],
        out_specs=[row(D_MODEL), row(D_MODEL), row(LANES), pl.BlockSpec((1, 1, LANES), lambda i: (i, 0, 0))],
        out_shape=[jax.ShapeDtypeStruct((cfg.t, D_MODEL), F32),
                   jax.ShapeDtypeStruct((cfg.t, D_MODEL), BF16),
                   jax.ShapeDtypeStruct((cfg.t, LANES), I32),
                   jax.ShapeDtypeStruct((n, 1, LANES), I32)],
        compiler_params=_cparams(),
        name="post_route",
    )(dn_f, dn_b, proj, ret_f, ret_b, proj, att, *xs, w_out.astype(BF16),
      jnp.tile(dn_norm, DN_HEADS).reshape(1, 256), ret_norm.reshape(1, 256),
      norm2.reshape(1, D_MODEL), rw2, rb, ltri)


def _route_tables(cnt, n_rows):
    n = cnt.shape[0]
    c8 = (cnt[:, :N_EXPERTS] + 7) // 8 * 8
    lstart = jnp.cumsum(c8, axis=1) - c8
    tot = jnp.sum(c8, axis=0)
    region = (tot + MOE_XB - 1) // MOE_XB * MOE_XB
    pend = jnp.cumsum(region)
    gstart = (pend - region)[None, :] + jnp.cumsum(c8, axis=0) - c8
    lrow = jnp.arange(MOE_NG) * SUBLANES
    seg_end = (lstart + c8)[:, None, :]
    e = jnp.sum((lrow[None, :, None] >= seg_end).astype(I32), axis=2)
    hot = e[:, :, None] == jnp.arange(N_EXPERTS)[None, None, :]
    g = jnp.sum(jnp.where(hot, (gstart - lstart)[:, None, :], 0), axis=2) + lrow[None, :]
    dump = n_rows - MOE_DUMP + lrow
    gdest = jnp.where(e < N_EXPERTS, g, dump[None, :]).reshape(-1).astype(I32)
    n_blocks = n_rows // MOE_XB
    nvalid = (pend[-1] // MOE_XB).astype(I32)
    bstart = jnp.minimum(jnp.arange(n_blocks), nvalid - 1) * MOE_XB
    block_exp = jnp.minimum(jnp.sum((pend[None, :] <= bstart[:, None]).astype(I32), axis=1),
                            N_EXPERTS - 1).astype(I32)
    last_blk = jnp.where(region > 0, pend - MOE_XB, -1).astype(I32)
    lstart_p = jnp.zeros((n, 1, LANES), I32).at[:, 0, :N_EXPERTS].set(lstart.astype(I32))
    return lstart_p, gdest, block_exp, last_blk, nvalid.reshape(1)


def _moe_rows(cfg):
    n_tiles = cfg.t // MOE_TM
    worst = TOP_K * cfg.t + n_tiles * N_EXPERTS * 7 + N_EXPERTS * (MOE_XB - 1)
    return -(-worst // MOE_XB) * MOE_XB + MOE_DUMP


def _local_dest(rt, lstart_row):
    lane = lax.broadcasted_iota(I32, rt.shape, 1)
    ls = lstart_row.astype(F32)
    out = []
    for k in range(TOP_K):
        eid = rt[:, RT_ID + k:RT_ID + k + 1]
        base = jnp.sum(jnp.where(lane == eid, ls, 0.0), axis=-1, keepdims=True).astype(I32)
        out.append(base + rt[:, RT_RANK + k:RT_RANK + k + 1])
    return out


def _group_copy(src, dst, sem, s, d):
    return pltpu.make_async_copy(src.at[pl.ds(s, SUBLANES)], dst.at[pl.ds(d, SUBLANES)], sem)


def _dispatch_kernel(n_blocks, gd_ref, lb_ref, nv_ref, h2_ref, rt_ref, ls_ref, xin_ref, xs_ref, sem):
    i = pl.program_id(0)

    @pl.when(i == 0)
    def _():
        xs_ref[0:MOE_XB, :] = jnp.zeros((MOE_XB, D_MODEL), F32)

        def zero_copy(d):
            return pltpu.make_async_copy(xs_ref.at[pl.ds(0, MOE_XB)],
                                         xin_ref.at[pl.ds(pl.multiple_of(d, MOE_XB), MOE_XB)], sem)

        def per_expert(fn):
            def body(e, c):
                d = lb_ref[e]

                @pl.when(d >= 0)
                def _():
                    fn(zero_copy(d))
                return c
            lax.fori_loop(0, N_EXPERTS, body, 0)

        def per_tail(fn):
            def body(b, c):
                fn(zero_copy(b * MOE_XB))
                return c
            lax.fori_loop(nv_ref[0], n_blocks, body, 0)

        per_expert(lambda cp: cp.start())
        per_tail(lambda cp: cp.start())
        per_expert(lambda cp: cp.wait())
        per_tail(lambda cp: cp.wait())

    rt_t = lax.bitcast_convert_type(lax.bitcast_convert_type(rt_ref[...], F32).T, I32)
    expert = lax.broadcasted_iota(I32, (LANES, MOE_TM), 0)
    ls_row = ls_ref[0].astype(F32)
    ld = []
    for k in range(TOP_K):
        onehot = jnp.where(expert == rt_t[RT_ID + k:RT_ID + k + 1, :], 1.0, 0.0).astype(BF16)
        base = _dot_x2(ls_row, onehot)
        ld.append(base.astype(I32) + rt_t[RT_RANK + k:RT_RANK + k + 1, :])
    h2 = h2_ref[...]
    for c in range(MOE_R // MOE_RC):
        slot = c * MOE_RC + lax.broadcasted_iota(I32, (MOE_RC, MOE_TM), 0)
        hit = (slot == ld[0]) | (slot == ld[1]) | (slot == ld[2]) | (slot == ld[3])
        xs_ref[c * MOE_RC:(c + 1) * MOE_RC, :] = _dot(jnp.where(hit, 1.0, 0.0).astype(BF16), h2)

    def start(j, c):
        d = gd_ref[i * MOE_NG + j]
        _group_copy(xs_ref, xin_ref, sem, pl.multiple_of(j * SUBLANES, SUBLANES),
                    pl.multiple_of(d, SUBLANES)).start()
        return c

    lax.fori_loop(0, MOE_NG, start, 0, unroll=DMA_UNROLL)
    pltpu.make_async_copy(xs_ref, xin_ref.at[pl.ds(0, MOE_R)], sem).wait()


def _dispatch(cfg, h2, rt, lstart, gdest, last_blk, nvalid, n_rows):
    tm = MOE_TM
    n = cfg.t // tm
    grid_spec = pltpu.PrefetchScalarGridSpec(
        num_scalar_prefetch=3,
        grid=(n,),
        in_specs=[pl.BlockSpec((tm, D_MODEL), lambda i, *_: (i, 0)),
                  pl.BlockSpec((tm, LANES), lambda i, *_: (i, 0)),
                  pl.BlockSpec((1, 1, LANES), lambda i, *_: (i, 0, 0))],
        out_specs=pl.BlockSpec(memory_space=pl.ANY),
        scratch_shapes=[pltpu.VMEM((MOE_R, D_MODEL), F32), pltpu.SemaphoreType.DMA(())],
    )
    return pl.pallas_call(
        functools.partial(_dispatch_kernel, n_rows // MOE_XB),
        grid_spec=grid_spec,
        out_shape=jax.ShapeDtypeStruct((n_rows, D_MODEL), F32),
        compiler_params=_cparams(),
        name="moe_dispatch",
    )(gdest, last_blk, nvalid, h2, rt, lstart)


def _expert_kernel(layer, be_ref, nv_ref, x_ref, wg_ref, bg_ref, wu_ref, bu_ref, wd_ref, bd_ref,
                   y_ref, wgc, wuc, wdc):
    del layer
    b = pl.program_id(0)
    prev = be_ref[jnp.maximum(b - 1, 0)]

    @pl.when((b == 0) | (be_ref[b] != prev))
    def _():
        wgc[...] = wg_ref[...].astype(BF16)
        wuc[...] = wu_ref[...].astype(BF16)
        wdc[...] = wd_ref[...].astype(BF16)

    @pl.when(b < nv_ref[0])
    def _():
        xb = x_ref[...].astype(BF16)
        gt = jnp.minimum(_dot(xb, wgc[...]) + bg_ref[...], SWIGLU_LIMIT)
        up = jnp.clip(_dot(xb, wuc[...]) + bu_ref[...], -SWIGLU_LIMIT, SWIGLU_LIMIT)
        hid = (up + 1.0) * gt * _sigmoid(SWIGLU_ALPHA * gt)
        y_ref[...] = _dot(hid.astype(BF16), wdc[...]) + bd_ref[...]

    @pl.when(b >= nv_ref[0])
    def _():
        y_ref[...] = jnp.zeros_like(y_ref)


def _experts(layer, xin, block_exp, nvalid, w_gate, b_gate, w_up, b_up, w_down, b_down):
    n_blocks = xin.shape[0] // MOE_XB
    d = D_MODEL
    nl, ne = w_gate.shape[0], w_gate.shape[1]
    xmap = lambda b, be, nv: (jnp.minimum(b, nv[0] - 1), 0)
    wmap = lambda b, be, nv: (layer, be[b], 0, 0)
    wspec = pl.BlockSpec((None, None, d, d), wmap)
    bspec = pl.BlockSpec((None, None, 1, d), wmap)
    grid_spec = pltpu.PrefetchScalarGridSpec(
        num_scalar_prefetch=2,
        grid=(n_blocks,),
        in_specs=[pl.BlockSpec((MOE_XB, d), xmap), wspec, bspec, wspec, bspec, wspec, bspec],
        out_specs=pl.BlockSpec((MOE_XB, d), lambda b, be, nv: (b, 0)),
        scratch_shapes=[pltpu.VMEM((d, d), BF16)] * 3,
    )
    return pl.pallas_call(
        functools.partial(_expert_kernel, layer),
        grid_spec=grid_spec,
        out_shape=jax.ShapeDtypeStruct(xin.shape, F32),
        compiler_params=_cparams(),
        name="moe_experts",
    )(block_exp, nvalid, xin, w_gate, b_gate.reshape(nl, ne, 1, d), w_up, b_up.reshape(nl, ne, 1, d),
      w_down, b_down.reshape(nl, ne, 1, d))


def _combine_kernel(cfg, final, gd_ref, rt_ref, ls_ref, x1_ref, pa_ref, pb_ref, pg_ref, pp_ref,
                    pn_ref, fn_ref, y_ref, *refs):
    o_refs, (ys_ref, sem) = refs[:-2], refs[-2:]
    i = pl.program_id(0)

    def start(j, c):
        d = gd_ref[i * MOE_NG + j]
        _group_copy(y_ref, ys_ref, sem, pl.multiple_of(d, SUBLANES),
                    pl.multiple_of(j * SUBLANES, SUBLANES)).start()
        return c

    lax.fori_loop(0, MOE_NG, start, 0, unroll=DMA_UNROLL)
    rt = rt_ref[...]
    ld = _local_dest(rt, ls_ref[0])
    gates = [lax.bitcast_convert_type(rt[:, RT_GATE + k:RT_GATE + k + 1], F32) for k in range(TOP_K)]
    pltpu.make_async_copy(y_ref.at[pl.ds(0, MOE_R)], ys_ref, sem).wait()
    x2 = x1_ref[...]
    for c in range(MOE_R // MOE_RC):
        slot = c * MOE_RC + lax.broadcasted_iota(I32, (MOE_TM, MOE_RC), 1)
        ptg = jnp.zeros((MOE_TM, MOE_RC), F32)
        for k in range(TOP_K):
            ptg = jnp.where(slot == ld[k], gates[k], ptg)
        x2 = x2 + _dot(ptg.astype(BF16), ys_ref[c * MOE_RC:(c + 1) * MOE_RC, :].astype(BF16))
    gate = _sigmoid(_dot(x2.astype(BF16), pg_ref[...]))
    p = _group_load(cfg, MOE_TM, (pa_ref, pb_ref))
    e = _dot(p.astype(BF16), pp_ref[...]) * gate
    x3 = x2 + _rms(e, pn_ref[...])
    if not final:
        o_refs[0][...] = x3
    else:
        x3 = _rms(x3, fn_ref[...])
        is_prompt = i < cfg.t0 // MOE_TM

        @pl.when(is_prompt)
        def _():
            o_refs[0][...] = x3

        @pl.when(jnp.logical_not(is_prompt))
        def _():
            o_refs[1][...] = x3


def _combine(cfg, layer, final, rt, lstart, gdest, x1, p_prompt, p_sample, ple_gate, ple_proj,
             ple_norm, final_norm, y):
    tm = MOE_TM
    n = cfg.t // tm
    row = lambda w: pl.BlockSpec((tm, w), lambda i, gd: (i, 0))
    const = lambda *s: pl.BlockSpec(s, lambda i, gd: (0,) * len(s))
    if final:
        out_specs = _group_specs(cfg, tm, D_MODEL, 2)
        out_shape = [jax.ShapeDtypeStruct((cfg.t0, D_MODEL), F32),
                     jax.ShapeDtypeStruct((cfg.t1, D_MODEL), F32)]
    else:
        out_specs = [row(D_MODEL)]
        out_shape = [jax.ShapeDtypeStruct((cfg.t, D_MODEL), F32)]
    grid_spec = pltpu.PrefetchScalarGridSpec(
        num_scalar_prefetch=1,
        grid=(n,),
        in_specs=[row(LANES), pl.BlockSpec((1, 1, LANES), lambda i, gd: (i, 0, 0)), row(D_MODEL)]
                 + _group_specs(cfg, tm, PLE_DIM, 2, lead=layer)
                 + [const(D_MODEL, D_MODEL), const(PLE_DIM, D_MODEL),
                    const(1, D_MODEL), const(1, D_MODEL), pl.BlockSpec(memory_space=pl.ANY)],
        out_specs=out_specs,
        scratch_shapes=[pltpu.VMEM((MOE_R, D_MODEL), F32), pltpu.SemaphoreType.DMA(())],
    )
    outs = pl.pallas_call(
        functools.partial(_combine_kernel, cfg, final),
        grid_spec=grid_spec,
        out_shape=out_shape,
        compiler_params=_cparams(),
        name="moe_combine",
    )(gdest, rt, lstart, x1, p_prompt, p_sample, ple_gate.astype(BF16), ple_proj.astype(BF16),
      ple_norm.reshape(1, D_MODEL), final_norm.reshape(1, D_MODEL), y)
    return outs if final else outs[0]


def _layer(cfg, layer, final, x, p_prompt, p_sample, tabs, norm1, w_in, conv_w, dn_a_log,
           dn_dt_bias, dn_norm, ret_decay, ret_norm, w_out, norm2, router_w, router_b, w_gate,
           b_gate, w_up, b_up, w_down, b_down, ple_proj, ple_gate, ple_norm, final_norm):
    proj = _in_proj(cfg, x, norm1[layer], _reorder_w_in(w_in[layer]), tabs[0], tabs[1], PROJ_TM)
    qkv, gates = _dn_prep(cfg, proj, conv_w[layer], dn_a_log[layer], dn_dt_bias[layer], PROJ_TM)
    dn_f, dn_b = _dn_scan(cfg, qkv, gates, SCAN_TM)
    ret_f, ret_b = _ret_scan(cfg, proj, _ret_tables(ret_decay[layer]), SCAN_TM)
    att = _attention(cfg, proj)
    x1, h2, rt, cnt = _post(cfg, dn_f, dn_b, ret_f, ret_b, att, proj, x, w_out[layer],
                            dn_norm[layer], ret_norm[layer], norm2[layer], router_w[layer],
                            router_b[layer])
    n_rows = _moe_rows(cfg)
    lstart, gdest, block_exp, last_blk, nvalid = _route_tables(cnt[:, 0, :], n_rows)
    xin = _dispatch(cfg, h2, rt, lstart, gdest, last_blk, nvalid, n_rows)
    y = _experts(layer, xin, block_exp, nvalid, w_gate, b_gate, w_up, b_up, w_down, b_down)
    return _combine(cfg, layer, final, rt, lstart, gdest, x1, p_prompt, p_sample, ple_gate[layer],
                    ple_proj[layer], ple_norm[layer], final_norm, y)


def _trunk(cfg, x_parts, p_prompt, p_sample, final_norm, params):
    tabs = _rope_tables(cfg.smax)
    depth = p_prompt.shape[0]
    x = x_parts
    for layer in range(depth):
        final = layer == depth - 1
        out = _layer(cfg, layer, final, x, p_prompt, p_sample, tabs, *params, final_norm)
        x = out if final else (out,)
    return x


def kernel(x_prompt, x_sample, p_prompt, p_sample, norm1, w_in, conv_w, dn_a_log, dn_dt_bias,
           dn_norm, ret_decay, ret_norm, w_out, norm2, router_w, router_b, w_gate, b_gate,
           w_up, b_up, w_down, b_down, ple_proj, ple_gate, ple_norm, final_norm):
    b0, s0, d = x_prompt.shape
    b1, s1, _ = x_sample.shape
    cfg = Cfg(b0, s0, b1, s1)
    depth = p_prompt.shape[0]
    params = (norm1, w_in, conv_w, dn_a_log, dn_dt_bias, dn_norm, ret_decay, ret_norm, w_out,
              norm2, router_w, router_b, w_gate, b_gate, w_up, b_up, w_down, b_down,
              ple_proj, ple_gate, ple_norm)
    y0, y1 = _trunk(cfg, (x_prompt.reshape(cfg.t0, d), x_sample.reshape(cfg.t1, d)),
                    p_prompt.reshape(depth, cfg.t0, PLE_DIM), p_sample.reshape(depth, cfg.t1, PLE_DIM),
                    final_norm, params)
    return (y0.reshape(b0, s0, d), y1.reshape(b1, s1, d))
```

```python
import functools
import math

import jax
import jax.numpy as jnp
import numpy as np
from jax import lax
from jax.experimental import pallas as pl
from jax.experimental.pallas import tpu as pltpu

F32 = jnp.float32
BF16 = jnp.bfloat16
I32 = jnp.int32

D_MODEL = 1024
HEAD_DIM = 64
DN_HEADS = 4
RET_HEADS = 4
ATT_HEADS = 8
DN_WIDTH = 256
RET_WIDTH = 256
ATT_WIDTH = 512
CONV_W = 4
RET_THETA = 10000.0
ROPE_THETA = 500000.0
ROPE_DIM = HEAD_DIM // 4
DILATED_PATTERNS = ((128, 1), (512, 4), (2048, 16))
ATT_HALF = 64
N_EXPERTS = 32
TOP_K = 4
SWIGLU_ALPHA = 1.702
SWIGLU_LIMIT = 7.0
PLE_DIM = 256
NORM_EPS = 1e-6
NEG_BIG = -1e30

LANES = 128
SUBLANES = 8
VMEM_LIMIT = 56 * 1024 * 1024

COL_DN_QKV = 0
COL_DN_Z = 768
COL_RQ = 1024
COL_RK = 1280
COL_RV = 1536
COL_RG = 1792
COL_AQ = 2048
COL_AK = 2560
COL_AV = 3072
COL_AB = 3584
PROJ_W = 3840

PROJ_TM = 512
SCAN_TM = 256
CHUNK = 64
CAT = 4 * CHUNK


class Cfg:
    def __init__(self, b0, s0, b1, s1):
        self.b0, self.s0, self.b1, self.s1 = b0, s0, b1, s1
        self.t0 = b0 * s0
        self.t1 = b1 * s1
        self.t = self.t0 + self.t1
        self.smax = max(s0, s1)

    def pos(self, row):
        return jnp.where(row < self.t0, row % self.s0, (row - self.t0) % self.s1)

    def seq_len(self, row):
        return jnp.where(row < self.t0, self.s0, self.s1)


def _cparams(n_axes=1, sem=None):
    return pltpu.CompilerParams(
        dimension_semantics=tuple(sem or ("arbitrary",) * n_axes),
        vmem_limit_bytes=VMEM_LIMIT)


def _group_specs(cfg, tm, width, n_parts, lead=None):
    pre_b = () if lead is None else (None,)
    pre_i = () if lead is None else (lead,)
    if n_parts == 1:
        return [pl.BlockSpec(pre_b + (tm, width), lambda i, *_: pre_i + (i, 0))]
    n0 = cfg.t0 // tm
    return [pl.BlockSpec(pre_b + (tm, width), lambda i, *_: pre_i + (jnp.minimum(i, n0 - 1), 0)),
            pl.BlockSpec(pre_b + (tm, width), lambda i, *_: pre_i + (jnp.maximum(i - n0, 0), 0))]


def _group_load(cfg, tm, refs, rows=slice(None)):
    if len(refs) == 1:
        return refs[0][rows, :]
    return jnp.where(pl.program_id(0) < cfg.t0 // tm, refs[0][rows, :], refs[1][rows, :])


def _split_bf16(x):
    hi = x.astype(BF16)
    lo = (x - hi.astype(F32)).astype(BF16)
    return hi, lo


def _dot(a, b):
    return jnp.dot(a, b, preferred_element_type=F32)


def _dot_x2(a, b_exact):
    hi, lo = _split_bf16(a)
    return _dot(hi, b_exact) + _dot(lo, b_exact)


def _dot_lx2(a_exact, b):
    hi, lo = _split_bf16(b)
    return _dot(a_exact, hi) + _dot(a_exact, lo)


def _dot_x3(a, b):
    ah, al = _split_bf16(a)
    bh, bl = _split_bf16(b)
    return _dot(ah, bh) + (_dot(al, bh) + _dot(ah, bl))


def _sigmoid(x):
    return 0.5 * jnp.tanh(0.5 * x) + 0.5


def _silu(x):
    return x * _sigmoid(x)


def _head_sumsq(x, hsum):
    return _dot((x * x).astype(BF16), hsum)


def _head_sum_matrix(width):
    r = lax.broadcasted_iota(I32, (width, width), 0) // HEAD_DIM
    c = lax.broadcasted_iota(I32, (width, width), 1) // HEAD_DIM
    return jnp.where(r == c, 1.0, 0.0).astype(BF16)


_SECTIONS = tuple((c, 1 if c in (COL_RQ, COL_RK) else 2 if COL_AQ <= c < COL_AV else 0)
                  for c in range(0, PROJ_W, 256))


def _rope_tables(smax):
    pos = jnp.arange(smax, dtype=F32)[:, None]
    d = jnp.arange(256) % HEAD_DIM

    def build(inv, half):
        rot = 2 * half
        ang = pos * inv[d % half][None, :]
        cos, sin = jnp.cos(ang), jnp.sin(ang)
        in_rot = (d < rot)[None, :]
        first = (d < half)[None, :]
        c = jnp.where(in_rot, cos, 1.0)
        s_plus = jnp.where(in_rot & ~first, sin, 0.0)
        s_minus = jnp.where(first, -sin, 0.0)
        return jnp.stack([c, s_plus, s_minus])

    inv_ret = 1.0 / jnp.power(jnp.float32(RET_THETA), jnp.linspace(0.0, 1.0, HEAD_DIM // 2, dtype=F32))
    inv_att = jnp.power(jnp.float32(ROPE_THETA), -jnp.arange(0, ROPE_DIM, 2, dtype=F32) / ROPE_DIM)
    return build(inv_ret, HEAD_DIM // 2), build(inv_att, ROPE_DIM // 2)


def _in_proj_kernel(cfg, tm, nx, *refs):
    g_ref, w_ref, tr_ref, ta_ref, o_ref = refs[nx:]
    x = _group_load(cfg, tm, refs[:nx])
    y = x * lax.rsqrt(jnp.mean(x * x, axis=-1, keepdims=True) + NORM_EPS)
    h = (y * g_ref[...]).astype(BF16)
    for col, kind in _SECTIONS:
        acc = _dot(h, w_ref[:, col:col + 256])
        if kind:
            t_ref, half = (tr_ref, HEAD_DIM // 2) if kind == 1 else (ta_ref, ROPE_DIM // 2)
            acc = (acc * t_ref[0] + pltpu.roll(acc, half, 1) * t_ref[1]
                   + pltpu.roll(acc, 256 - half, 1) * t_ref[2])
        o_ref[:, col:col + 256] = acc


def _in_proj(cfg, xs, g1, w_bf16, tab_ret, tab_att, tm):
    n = cfg.t // tm

    def tab_map(i):
        return (0, cfg.pos(i * tm) // tm, 0)

    return pl.pallas_call(
        functools.partial(_in_proj_kernel, cfg, tm, len(xs)),
        grid=(n,),
        in_specs=_group_specs(cfg, tm, D_MODEL, len(xs)) + [
            pl.BlockSpec((1, D_MODEL), lambda i: (0, 0)),
            pl.BlockSpec((D_MODEL, PROJ_W), lambda i: (0, 0)),
            pl.BlockSpec((3, tm, 256), tab_map),
            pl.BlockSpec((3, tm, 256), tab_map),
        ],
        out_specs=pl.BlockSpec((tm, PROJ_W), lambda i: (i, 0)),
        out_shape=jax.ShapeDtypeStruct((cfg.t, PROJ_W), F32),
        compiler_params=_cparams(),
        name="in_proj",
    )(*xs, g1.reshape(1, D_MODEL), w_bf16, tab_ret, tab_att)


def _reorder_w_in(w_in):
    ab = w_in[:, 1024:1040]
    rest = w_in[:, 1040:]
    pad = jnp.zeros((w_in.shape[0], PROJ_W - 3600), w_in.dtype)
    return jnp.concatenate([w_in[:, :1024], rest, ab, pad], axis=1).astype(BF16)


def _dn_prep_kernel(cfg, tm, cur_ref, prev_ref, next_ref, ab_ref, cw_ref, ga_ref, gb_ref,
                    qkv_ref, gate_ref, ext_ref):
    i = pl.program_id(0)
    pos = cfg.pos(i * tm)
    first = pos == 0
    last = pos + tm == cfg.seq_len(i * tm)
    ext_ref[0:8, :] = jnp.where(first, 0.0, prev_ref[...])
    ext_ref[8:8 + tm, :] = cur_ref[...]
    ext_ref[8 + tm:16 + tm, :] = jnp.where(last, 0.0, next_ref[...])
    y = cw_ref[0:1, :] * ext_ref[6:6 + tm, :]
    for j in range(1, CONV_W):
        y = y + cw_ref[j:j + 1, :] * ext_ref[6 + j:6 + j + tm, :]
    y = _silu(y)
    hsum = _head_sum_matrix(DN_WIDTH)
    q = y[:, 0:DN_WIDTH]
    k = y[:, DN_WIDTH:2 * DN_WIDTH]
    qkv_ref[:, 0:DN_WIDTH] = q * lax.rsqrt(_head_sumsq(q, hsum) + NORM_EPS) * (HEAD_DIM ** -0.5)
    qkv_ref[:, DN_WIDTH:2 * DN_WIDTH] = k * lax.rsqrt(_head_sumsq(k, hsum) + NORM_EPS)
    qkv_ref[:, 2 * DN_WIDTH:] = y[:, 2 * DN_WIDTH:]
    ab = ab_ref[:, 0:LANES]
    xa = ab + gb_ref[...]
    softplus = jnp.maximum(xa, 0.0) + jnp.log(1.0 + jnp.exp(-jnp.abs(xa)))
    lane = lax.broadcasted_iota(I32, ab.shape, 1)
    gate_ref[...] = jnp.where(lane < 2 * DN_HEADS, ga_ref[...] * softplus, _sigmoid(ab))


def _dn_prep(cfg, proj, conv_w, a_log, dt_bias, tm):
    n = cfg.t // tm
    w3 = 3 * DN_WIDTH
    nh = tm // 8
    last8 = cfg.t // 8 - 1
    neg_a = jnp.zeros((1, LANES), F32).at[0, :2 * DN_HEADS].set(-jnp.exp(a_log.reshape(-1)))
    dtb = jnp.zeros((1, LANES), F32).at[0, :2 * DN_HEADS].set(dt_bias.reshape(-1))
    return pl.pallas_call(
        functools.partial(_dn_prep_kernel, cfg, tm),
        grid=(n,),
        in_specs=[
            pl.BlockSpec((tm, w3), lambda i: (i, 0)),
            pl.BlockSpec((8, w3), lambda i: (jnp.maximum(i * nh - 1, 0), 0)),
            pl.BlockSpec((8, w3), lambda i: (jnp.minimum((i + 1) * nh, last8), 0)),
            pl.BlockSpec((tm, 256), lambda i: (i, COL_AB // 256)),
            pl.BlockSpec((CONV_W, w3), lambda i: (0, 0)),
            pl.BlockSpec((1, LANES), lambda i: (0, 0)),
            pl.BlockSpec((1, LANES), lambda i: (0, 0)),
        ],
        out_specs=[
            pl.BlockSpec((tm, w3), lambda i: (i, 0)),
            pl.BlockSpec((tm, LANES), lambda i: (i, 0)),
        ],
        out_shape=[jax.ShapeDtypeStruct((cfg.t, w3), F32),
                   jax.ShapeDtypeStruct((cfg.t, LANES), F32)],
        scratch_shapes=[pltpu.VMEM((tm + 16, w3), F32)],
        compiler_params=_cparams(),
        name="dn_prep",
    )(proj, proj, proj, proj, conv_w, neg_a, dtb)


def _bd_mask():
    r = lax.broadcasted_iota(I32, (CAT, CAT), 0) // CHUNK
    c = lax.broadcasted_iota(I32, (CAT, CAT), 1) // CHUNK
    return r == c, jnp.where(r == c, 1.0, 0.0).astype(BF16)


def _tile4(x):
    xb = x.astype(BF16)
    return jnp.concatenate([xb, xb, xb, xb], axis=0)


def _bd(x, bdmask):
    return _tile4(x) * bdmask


def _dot_t0(a, b):
    return lax.dot_general(a, b, (((0,), (0,)), ((), ())), preferred_element_type=F32)


def _dot_t1(a, b):
    return lax.dot_general(a, b, (((1,), (1,)), ((), ())), preferred_element_type=F32)


def _chunk_ij(rows):
    i = lax.broadcasted_iota(I32, (rows, CAT), 0) % CHUNK
    j = lax.broadcasted_iota(I32, (rows, CAT), 1) % CHUNK
    return i, j


def _segments(cfg, tm, want):
    for nseg in (want, 2, 1):
        seg = cfg.t // nseg
        if cfg.t % nseg == 0 and seg % tm == 0 and all(
                (s * seg) % cfg.s0 == 0 if s * seg <= cfg.t0 else (s * seg - cfg.t0) % cfg.s1 == 0
                for s in range(nseg)):
            return nseg
    return 1


def _stream_rows(nseg, per, tm):
    fns = []
    for s in range(nseg):
        fns.append(lambda i, s=s: (s * per + i) * tm)
        fns.append(lambda i, s=s: (s * per + per - 1 - i) * tm)
    return fns


def _stream_views(nseg, fwd_ref, bwd_ref):
    out = []
    for s in range(nseg):
        out += [fwd_ref.at[s], bwd_ref.at[s]]
    return out


def _dn_streams(revs, qkv_refs, gate_refs, o_refs, s_refs, tm):
    nc = tm // CHUNK
    bd_bool, bdmask = _bd_mask()
    i, j = _chunk_ij(tm)
    ic, jc = _chunk_ij(CHUNK)
    eye = jnp.where(ic == jc, 1.0, 0.0)
    lvl_masks = {lvl: _bd(jnp.where(((ic >> lvl) == (jc >> lvl)) & ((ic >> (lvl - 1)) != (jc >> (lvl - 1))),
                                    1.0, 0.0), bdmask) for lvl in range(2, 7)}
    l = lax.broadcasted_iota(I32, (LANES, 2 * CAT), 0)
    c = lax.broadcasted_iota(I32, (LANES, 2 * CAT), 1)
    ti = lax.broadcasted_iota(I32, (tm, tm), 0)
    tk = lax.broadcasted_iota(I32, (tm, tm), 1)
    same = (ti // CHUNK) == (tk // CHUNK)
    consts = {}
    for rev in sorted(set(revs)):
        base = (4 if rev else 0) + jnp.where(c >= CAT, 2 * DN_HEADS, 0)
        consts[rev] = dict(
            sel=jnp.where(l == base + (c % CAT) // CHUNK, 1.0, 0.0).astype(BF16),
            tri=jnp.where(same & ((tk >= ti) if rev else (tk <= ti)), 1.0, 0.0).astype(BF16),
            between=(i < j) if rev else (i > j),
            incl=(i <= j) if rev else (i >= j),
            strict=(ic < jc) if rev else (ic > jc))

    st = []
    for rev, qkv_ref, gate_ref in zip(revs, qkv_refs, gate_refs):
        cs = consts[rev]
        gb = _dot_x2(gate_ref[...], cs["sel"])
        g_cat, beta = gb[:, :CAT], gb[:, CAT:]
        gc = _dot_lx2(cs["tri"], g_cat)
        diff = _dot_lx2(cs["tri"], jnp.where(cs["between"], g_cat, 0.0))
        st.append(dict(rev=rev, beta=beta, gc=gc, egc=jnp.exp(gc),
                       decay=jnp.where(cs["incl"], jnp.exp(diff), 0.0),
                       q=qkv_ref[:, 0:CAT], k=qkv_ref[:, CAT:2 * CAT], v=qkv_ref[:, 2 * CAT:3 * CAT]))

    chains = [(si, ci) for si in range(len(st)) for ci in range(nc)]
    attn, amat, tinv = {}, {}, {}
    for si, ci in chains:
        s_, r = st[si], slice(ci * CHUNK, (ci + 1) * CHUNK)
        qk_kk = _dot_t1(jnp.concatenate([s_["q"][r], s_["k"][r]], axis=0).astype(BF16),
                        _bd(s_["k"][r], bdmask))
        attn[si, ci] = qk_kk[:CHUNK] * s_["decay"][r]
        a = jnp.where(consts[s_["rev"]]["strict"], qk_kk[CHUNK:] * s_["beta"][r] * s_["decay"][r], 0.0)
        amat[si, ci] = _tile4(a)
        tinv[si, ci] = eye - jnp.where((ic >> 1) == (jc >> 1), a, 0.0)
    for lvl in range(2, 7):
        xs = {key: _dot(tinv[key].astype(BF16), amat[key] * lvl_masks[lvl]) for key in chains}
        for key in chains:
            tinv[key] = tinv[key] - _dot(xs[key].astype(BF16), _bd(tinv[key], bdmask))

    states = [s_ref[...] for s_ref in s_refs]
    for step in range(nc):
        idx = [(nc - 1 - step) if s_["rev"] else step for s_ in st]
        rs = [slice(ci * CHUNK, (ci + 1) * CHUNK) for ci in idx]
        x1s = [_dot(jnp.concatenate([s_["k"][r] * s_["beta"][r] * s_["egc"][r], s_["q"][r] * s_["egc"][r]],
                                    axis=0).astype(BF16), state.astype(BF16))
               for s_, r, state in zip(st, rs, states)]
        vns = [_dot(tinv[si, ci].astype(BF16), _bd(s_["v"][r] * s_["beta"][r] - x1[:CHUNK], bdmask))
               for si, (s_, r, ci, x1) in enumerate(zip(st, rs, idx, x1s))]
        for si, (s_, r, ci, x1, vn, o_ref) in enumerate(zip(st, rs, idx, x1s, vns, o_refs)):
            o_ref[r, :] = x1[CHUNK:] + _dot(attn[si, ci].astype(BF16), _bd(vn, bdmask))
            gcc = s_["gc"][r]
            g_last = gcc[0:1] if s_["rev"] else gcc[CHUNK - 1:CHUNK]
            k_dec = s_["k"][r] * jnp.exp(g_last - gcc)
            states[si] = states[si] * jnp.exp(g_last) + jnp.where(
                bd_bool, _dot_t0(k_dec.astype(BF16), vn.astype(BF16)), 0.0)
    for s_ref, state in zip(s_refs, states):
        s_ref[...] = state


def _reset_states(cfg, tm, row_fns, s_refs):
    i = pl.program_id(0)
    for si, (row_fn, s_ref) in enumerate(zip(row_fns, s_refs)):
        row = row_fn(i)
        edge = (cfg.pos(row) + tm == cfg.seq_len(row)) if si % 2 else (cfg.pos(row) == 0)

        @pl.when(edge)
        def _(s_ref=s_ref):
            s_ref[...] = jnp.zeros_like(s_ref)


def _dn_scan_kernel(cfg, tm, nseg, per, qf_ref, gf_ref, qb_ref, gb_ref, of_ref, ob_ref, *s_refs):
    _reset_states(cfg, tm, _stream_rows(nseg, per, tm), s_refs)
    _dn_streams([si % 2 == 1 for si in range(2 * nseg)], _stream_views(nseg, qf_ref, qb_ref),
                _stream_views(nseg, gf_ref, gb_ref), _stream_views(nseg, of_ref, ob_ref), s_refs, tm)


def _scan_specs(nseg, per, tm, width, col_block=0):
    return (pl.BlockSpec((nseg, tm, width), lambda i: (0, i, col_block)),
            pl.BlockSpec((nseg, tm, width), lambda i: (0, per - 1 - i, col_block)))


def _dn_scan(cfg, qkv, gate, tm, want_segments=2):
    nseg = _segments(cfg, tm, want_segments)
    per = cfg.t // nseg // tm
    w3 = 3 * DN_WIDTH
    qf, qb = _scan_specs(nseg, per, tm, w3)
    gf, gb = _scan_specs(nseg, per, tm, LANES)
    of, ob = _scan_specs(nseg, per, tm, DN_WIDTH)
    qkv3 = qkv.reshape(nseg, cfg.t // nseg, w3)
    gate3 = gate.reshape(nseg, cfg.t // nseg, LANES)
    o_f, o_b = pl.pallas_call(
        functools.partial(_dn_scan_kernel, cfg, tm, nseg, per),
        grid=(per,),
        in_specs=[qf, gf, qb, gb],
        out_specs=[of, ob],
        out_shape=[jax.ShapeDtypeStruct((nseg, cfg.t // nseg, DN_WIDTH), F32)] * 2,
        scratch_shapes=[pltpu.VMEM((CAT, CAT), F32)] * (2 * nseg),
        compiler_params=_cparams(),
        name="dn_scan",
    )(qkv3, gate3, qkv3, gate3)
    return o_f.reshape(cfg.t, DN_WIDTH), o_b.reshape(cfg.t, DN_WIDTH)


def _ret_tables(decay_logit):
    lg = jnp.log(jax.nn.sigmoid(decay_logit.astype(F32)))
    i = jnp.arange(CHUNK, dtype=F32)[:, None]
    j = (jnp.arange(CAT) % CHUNK).astype(F32)[None, :]
    scale = HEAD_DIM ** -0.5
    out = []
    for d in range(2):
        lgl = jnp.repeat(lg[d], CHUNK)[None, :]
        rel = (j - i) if d else (i - j)
        dmat = jnp.where(rel >= 0, jnp.exp(jnp.where(rel >= 0, rel, 0.0) * lgl), 0.0) * scale
        qdec = jnp.exp(((CHUNK - i) if d else (i + 1.0)) * lgl)
        kdec = jnp.exp((i if d else (CHUNK - 1.0 - i)) * lgl) * scale
        cd = jnp.broadcast_to(jnp.exp(CHUNK * lgl), (CHUNK, CAT))
        out.append(jnp.stack([dmat, qdec, kdec, cd]))
    return jnp.stack(out)


def _ret_streams(revs, q_refs, k_refs, v_refs, tab_ref, o_refs, s_refs, tm):
    nc = tm // CHUNK
    bd_bool, bdmask = _bd_mask()
    tabs = {rev: [tab_ref[1 if rev else 0, t] for t in range(4)] for rev in sorted(set(revs))}
    inner = {}
    for si, rev in enumerate(revs):
        dmat = tabs[rev][0]
        for ci in range(nc):
            r = slice(ci * CHUNK, (ci + 1) * CHUNK)
            scores = _dot_t1(q_refs[si][r, :].astype(BF16), _bd(k_refs[si][r, :], bdmask)) * dmat
            inner[si, ci] = _dot(scores.astype(BF16), _bd(v_refs[si][r, :], bdmask))
    states = [s_ref[...] for s_ref in s_refs]
    for step in range(nc):
        for si, rev in enumerate(revs):
            _, qdec, kdec, cd = tabs[rev]
            ci = (nc - 1 - step) if rev else step
            r = slice(ci * CHUNK, (ci + 1) * CHUNK)
            qc, kc, vc = q_refs[si][r, :], k_refs[si][r, :], v_refs[si][r, :]
            o_refs[si][r, :] = inner[si, ci] + _dot((qc * qdec).astype(BF16), states[si].astype(BF16))
            kv = _dot_t0((kc * kdec).astype(BF16), vc.astype(BF16))
            states[si] = states[si] * cd[0:1] + jnp.where(bd_bool, kv, 0.0)
    for s_ref, state in zip(s_refs, states):
        s_ref[...] = state


def _ret_scan_kernel(cfg, tm, nseg, per, qf, kf, vf, qb, kb, vb, tab_ref, of_ref, ob_ref, *s_refs):
    _reset_states(cfg, tm, _stream_rows(nseg, per, tm), s_refs)
    _ret_streams([si % 2 == 1 for si in range(2 * nseg)], _stream_views(nseg, qf, qb),
                 _stream_views(nseg, kf, kb), _stream_views(nseg, vf, vb), tab_ref,
                 _stream_views(nseg, of_ref, ob_ref), s_refs, tm)


def _ret_scan(cfg, proj, tables, tm, want_segments=2):
    nseg = _segments(cfg, tm, want_segments)
    per = cfg.t // nseg // tm
    proj3 = proj.reshape(nseg, cfg.t // nseg, PROJ_W)
    specs = [_scan_specs(nseg, per, tm, 256, col // 256) for col in (COL_RQ, COL_RK, COL_RV)]
    of, ob = _scan_specs(nseg, per, tm, RET_WIDTH)
    o_f, o_b = pl.pallas_call(
        functools.partial(_ret_scan_kernel, cfg, tm, nseg, per),
        grid=(per,),
        in_specs=[s[0] for s in specs] + [s[1] for s in specs]
                 + [pl.BlockSpec((2, 4, CHUNK, CAT), lambda i: (0, 0, 0, 0))],
        out_specs=[of, ob],
        out_shape=[jax.ShapeDtypeStruct((nseg, cfg.t // nseg, RET_WIDTH), F32)] * 2,
        scratch_shapes=[pltpu.VMEM((CAT, CAT), F32)] * (2 * nseg),
        compiler_params=_cparams(),
        name="ret_scan",
    )(proj3, proj3, proj3, proj3, proj3, proj3, tables)
    return o_f.reshape(cfg.t, RET_WIDTH), o_b.reshape(cfg.t, RET_WIDTH)


ATT_TQ = 2048
ATT_HALO = 1024
ATT_QB = 128


def _ds(start, size, stride):
    return pl.ds(start, size) if stride == 1 else pl.ds(start, size, stride=stride)


ATT_NB = 4


def _att_kernel(cfg, q_ref, kp_ref, kc_ref, kn_ref, vp_ref, vc_ref, vn_ref, o_ref,
                kbuf, vbuf, bias_sc, m_sc, l_sc, acc_sc):
    tq, halo, qb = ATT_TQ, ATT_HALO, ATT_QB
    kw = qb + 2 * ATT_HALF
    i = pl.program_id(1)
    pos0 = cfg.pos(i * tq)
    slen = cfg.seq_len(i * tq)
    kbuf[0:halo, :] = kp_ref[...]
    kbuf[halo:halo + tq, :] = kc_ref[...]
    kbuf[halo + tq:, :] = kn_ref[...]
    vbuf[0:halo, :] = vp_ref[...]
    vbuf[halo:halo + tq, :] = vc_ref[...]
    vbuf[halo + tq:, :] = vn_ref[...]
    head0 = lax.broadcasted_iota(I32, (qb, LANES), 1) < HEAD_DIM
    qi = lax.broadcasted_iota(I32, (qb, kw), 0)
    kj = lax.broadcasted_iota(I32, (qb, kw), 1)
    bias_sc[...] = jnp.where((kj - qi >= 0) & (kj - qi <= 2 * ATT_HALF), 0.0, NEG_BIG)
    krow = lax.broadcasted_iota(I32, (1, kw), 1)
    qscale = HEAD_DIM ** -0.5 * math.log2(math.e)
    ones_v = jnp.ones((kw, LANES), BF16)

    for p, (_, dil) in enumerate(DILATED_PATTERNS):
        def body(it, carry, p=p, dil=dil):
            where_q, where_k, scores = [], [], []
            for jj in range(ATT_NB):
                u = it * ATT_NB + jj
                r = u % dil
                b = u // dil
                qs = r + dil * qb * b
                ks = halo + r + dil * (qb * b - ATT_HALF)
                where_q.append(qs)
                where_k.append(ks)
                q = q_ref[_ds(qs, qb, dil), :] * qscale
                k = kbuf[_ds(ks, kw, dil), :].astype(BF16)
                kpos = pos0 // dil + qb * b - ATT_HALF + krow
                valid = jnp.where((kpos >= 0) & (kpos < slen // dil), 0.0, NEG_BIG)
                for h in range(2):
                    qh = jnp.where(head0 if h == 0 else ~head0, q, 0.0).astype(BF16)
                    scores.append(_dot_t1(qh, k) + bias_sc[...] + valid)
            ms = [jnp.max(s, axis=-1, keepdims=True) for s in scores]
            ps = [jnp.exp2(s - m).astype(BF16) for s, m in zip(scores, ms)]
            for jj in range(ATT_NB):
                v = vbuf[_ds(where_k[jj], kw, dil), :].astype(BF16)
                vext = jnp.concatenate([v, ones_v], axis=1)
                a0 = _dot(ps[2 * jj], vext)
                a1 = _dot(ps[2 * jj + 1], vext)
                rows = _ds(where_q[jj], qb, dil)
                m_sc[p, rows, :] = jnp.where(head0, ms[2 * jj], ms[2 * jj + 1])
                l_sc[p, rows, :] = jnp.where(head0, a0[:, LANES:], a1[:, LANES:])
                acc_sc[p, rows, :] = jnp.where(head0, a0[:, :LANES], a1[:, :LANES])
            return carry

        lax.fori_loop(0, tq // qb // ATT_NB, body, 0)

    m_all = [m_sc[p] for p in range(3)]
    mx = jnp.maximum(jnp.maximum(m_all[0], m_all[1]), m_all[2])
    num = jnp.zeros((tq, LANES), F32)
    den = jnp.zeros((tq, LANES), F32)
    for p in range(3):
        w = jnp.exp2(m_all[p] - mx)
        num = num + w * acc_sc[p]
        den = den + w * l_sc[p]
    o_ref[...] = num / den


def _attention(cfg, proj):
    tq, halo = ATT_TQ, ATT_HALO
    n = cfg.t // tq
    per = tq // halo
    nh = cfg.t // halo

    def cur(col):
        return pl.BlockSpec((tq, LANES), lambda hp, i: (i, col // LANES + hp))

    def prev(col):
        return pl.BlockSpec((halo, LANES), lambda hp, i: (jnp.maximum(i * per - 1, 0), col // LANES + hp))

    def nxt(col):
        return pl.BlockSpec((halo, LANES), lambda hp, i: (jnp.minimum((i + 1) * per, nh - 1), col // LANES + hp))

    return pl.pallas_call(
        functools.partial(_att_kernel, cfg),
        grid=(ATT_HEADS // 2, n),
        in_specs=[cur(COL_AQ), prev(COL_AK), cur(COL_AK), nxt(COL_AK),
                  prev(COL_AV), cur(COL_AV), nxt(COL_AV)],
        out_specs=pl.BlockSpec((tq, LANES), lambda hp, i: (i, hp)),
        out_shape=jax.ShapeDtypeStruct((cfg.t, ATT_WIDTH), F32),
        scratch_shapes=[pltpu.VMEM((tq + 2 * halo, LANES), F32)] * 2
                       + [pltpu.VMEM((ATT_QB, ATT_QB + 2 * ATT_HALF), F32)]
                       + [pltpu.VMEM((3, tq, LANES), F32)] * 3,
        compiler_params=_cparams(2),
        name="attention",
    )(proj, proj, proj, proj, proj, proj, proj)


MOE_TM = 512
MOE_R = 4 * MOE_TM + 256
MOE_NG = MOE_R // SUBLANES
MOE_RC = MOE_R // 3
MOE_XB = 512
MOE_DUMP = -(-MOE_R // MOE_XB) * MOE_XB
RT_ID, RT_RANK, RT_GATE = 0, 4, 8
POST_SUB = MOE_TM
DMA_UNROLL = 8


def _rms(x, g):
    return x * lax.rsqrt(jnp.mean(x * x, axis=-1, keepdims=True) + NORM_EPS) * g


def _post_kernel(cfg, nx, dnf, dnb, z_ref, rtf, rtb, rg_ref, att_ref, *refs):
    (wout_ref, dnn_ref, rtn_ref, n2_ref, rw_ref, rb_ref, ltri_ref,
     x1_ref, h2_ref, rt_ref, cnt_ref) = refs[nx:]
    hsum = _head_sum_matrix(DN_WIDTH)
    inv = 1.0 / HEAD_DIM
    tm, sub = MOE_TM, POST_SUB
    lane = lax.broadcasted_iota(I32, (sub, LANES), 1)
    lane_f = lane.astype(F32)
    parts = []
    for s in range(tm // sub):
        r = slice(s * sub, (s + 1) * sub)
        o = dnf[r, :] + dnb[r, :]
        dn = o * lax.rsqrt(_head_sumsq(o, hsum) * inv + NORM_EPS) * dnn_ref[...] * _silu(z_ref[r, :])
        o = rtf[r, :] + rtb[r, :]
        oc = o - _dot_x2(o, hsum) * inv
        ret = (oc * lax.rsqrt(_head_sumsq(oc, hsum) * inv + NORM_EPS) * rtn_ref[...]
               * _silu(rg_ref[r, :]))
        mixed = (_dot(dn.astype(BF16), wout_ref[0:256, :]) + _dot(ret.astype(BF16), wout_ref[256:512, :])
                 + _dot(att_ref[r, :].astype(BF16), wout_ref[512:1024, :]))
        x1 = _group_load(cfg, MOE_TM, refs[:nx], r) + mixed
        x1_ref[r, :] = x1
        h2 = _rms(x1, n2_ref[...])
        h2_ref[r, :] = h2.astype(BF16)
        hh, hl = _split_bf16(h2)
        work = _dot(hh, rw_ref[0]) + (_dot(hl, rw_ref[0]) + _dot(hh, rw_ref[1])) + rb_ref[...]
        vals, ids = [], []
        for _ in range(TOP_K):
            m = jnp.max(work, axis=-1, keepdims=True)
            idx = jnp.min(jnp.where(work == m, lane_f, float(LANES)), axis=-1, keepdims=True).astype(I32)
            vals.append(m)
            ids.append(idx)
            work = jnp.where(lane == idx, -3e38, work)
        es = [jnp.exp(v - vals[0]) for v in vals]
        onehots = [jnp.where(lane == idx, 1.0, 0.0) for idx in ids]
        parts.append(dict(ids=ids, es=es, tot=es[0] + es[1] + es[2] + es[3], onehots=onehots,
                          msum=onehots[0] + onehots[1] + onehots[2] + onehots[3]))
    msum = jnp.concatenate([pt["msum"] for pt in parts], axis=0)
    before = _dot(ltri_ref[...], msum.astype(BF16))
    for s, pt in enumerate(parts):
        r = slice(s * sub, (s + 1) * sub)
        rec = jnp.zeros((sub, LANES), I32)
        for k in range(TOP_K):
            rank = jnp.sum(pt["onehots"][k] * before[r], axis=-1, keepdims=True).astype(I32)
            gate_bits = lax.bitcast_convert_type(pt["es"][k] / pt["tot"], I32)
            rec = jnp.where(lane == RT_ID + k, pt["ids"][k], rec)
            rec = jnp.where(lane == RT_RANK + k, rank, rec)
            rec = jnp.where(lane == RT_GATE + k, gate_bits, rec)
        rt_ref[r, :] = rec
    cnt_ref[0] = (before[tm - 1:tm, :] + msum[tm - 1:tm, :]).astype(I32)


def _post(cfg, dn_f, dn_b, ret_f, ret_b, att, proj, xs, w_out, dn_norm, ret_norm, norm2,
          router_w, router_b):
    tm = MOE_TM
    n = cfg.t // tm
    rw = jnp.zeros((D_MODEL, LANES), F32).at[:, :N_EXPERTS].set(router_w)
    rw_hi = rw.astype(BF16)
    rw2 = jnp.stack([rw_hi, (rw - rw_hi.astype(F32)).astype(BF16)])
    rb = jnp.full((1, LANES), NEG_BIG, F32).at[0, :N_EXPERTS].set(router_b)
    ltri = (jnp.arange(tm)[:, None] > jnp.arange(tm)[None, :]).astype(BF16)
    row = lambda w: pl.BlockSpec((tm, w), lambda i: (i, 0))
    col = lambda c: pl.BlockSpec((tm, 256), lambda i: (i, c // 256))
    const = lambda *s: pl.BlockSpec(s, lambda i: (0,) * len(s))
    return pl.pallas_call(
        functools.partial(_post_kernel, cfg, len(xs)),
        grid=(n,),
        in_specs=[row(256), row(256), col(COL_DN_Z), row(256), row(256), col(COL_RG), row(512)]
                 + _group_specs(cfg, tm, D_MODEL, len(xs))
                 + [const(D_MODEL, D_MODEL), const(1, 256), const(1, 256),
                    const(1, D_MODEL), const(2, D_MODEL, LANES), const(1, LANES), const(tm, tm)],
        out_specs=[row(D_MODEL), row(D_MODEL), row(LANES), pl.BlockSpec((1, 1, LANES), lambda i: (i, 0, 0))],
        out_shape=[jax.ShapeDtypeStruct((cfg.t, D_MODEL), F32),
                   jax.ShapeDtypeStruct((cfg.t, D_MODEL), BF16),
                   jax.ShapeDtypeStruct((cfg.t, LANES), I32),
                   jax.ShapeDtypeStruct((n, 1, LANES), I32)],
        compiler_params=_cparams(),
        name="post_route",
    )(dn_f, dn_b, proj, ret_f, ret_b, proj, att, *xs, w_out.astype(BF16),
      jnp.tile(dn_norm, DN_HEADS).reshape(1, 256), ret_norm.reshape(1, 256),
      norm2.reshape(1, D_MODEL), rw2, rb, ltri)


def _route_tables(cnt, n_rows):
    n = cnt.shape[0]
    c8 = (cnt[:, :N_EXPERTS] + 7) // 8 * 8
    lstart = jnp.cumsum(c8, axis=1) - c8
    tot = jnp.sum(c8, axis=0)
    region = (tot + MOE_XB - 1) // MOE_XB * MOE_XB
    pend = jnp.cumsum(region)
    gstart = (pend - region)[None, :] + jnp.cumsum(c8, axis=0) - c8
    lrow = jnp.arange(MOE_NG) * SUBLANES
    seg_end = (lstart + c8)[:, None, :]
    e = jnp.sum((lrow[None, :, None] >= seg_end).astype(I32), axis=2)
    hot = e[:, :, None] == jnp.arange(N_EXPERTS)[None, None, :]
    g = jnp.sum(jnp.where(hot, (gstart - lstart)[:, None, :], 0), axis=2) + lrow[None, :]
    dump = n_rows - MOE_DUMP * (1 + jnp.arange(n) % 2)[:, None] + lrow[None, :]
    gdest = jnp.where(e < N_EXPERTS, g, dump).reshape(-1).astype(I32)
    n_blocks = n_rows // MOE_XB
    nvalid = (pend[-1] // MOE_XB).astype(I32)
    bstart = jnp.minimum(jnp.arange(n_blocks), nvalid - 1) * MOE_XB
    block_exp = jnp.minimum(jnp.sum((pend[None, :] <= bstart[:, None]).astype(I32), axis=1),
                            N_EXPERTS - 1).astype(I32)
    last_blk = jnp.where(region > 0, pend - MOE_XB, -1).astype(I32)
    lstart_p = jnp.zeros((n, 1, LANES), I32).at[:, 0, :N_EXPERTS].set(lstart.astype(I32))
    return lstart_p, gdest, block_exp, last_blk, nvalid.reshape(1)


def _moe_rows(cfg):
    n_tiles = cfg.t // MOE_TM
    worst = TOP_K * cfg.t + n_tiles * N_EXPERTS * 7 + N_EXPERTS * (MOE_XB - 1)
    return -(-worst // MOE_XB) * MOE_XB + 2 * MOE_DUMP


def _local_dest(rt, lstart_row):
    lane = lax.broadcasted_iota(I32, rt.shape, 1)
    ls = lstart_row.astype(F32)
    out = []
    for k in range(TOP_K):
        eid = rt[:, RT_ID + k:RT_ID + k + 1]
        base = jnp.sum(jnp.where(lane == eid, ls, 0.0), axis=-1, keepdims=True).astype(I32)
        out.append(base + rt[:, RT_RANK + k:RT_RANK + k + 1])
    return out


def _group_copy(src, dst, sem, s, d):
    return pltpu.make_async_copy(src.at[pl.ds(s, SUBLANES)], dst.at[pl.ds(d, SUBLANES)], sem)


def _dispatch_kernel(n_tiles, n_blocks, gd_ref, lb_ref, nv_ref, h2_ref, rt_ref, ls_ref, xin_ref,
                     xs_ref, sem):
    i = pl.program_id(0)
    par = i % 2

    def wait_tile(p):
        pltpu.make_async_copy(xs_ref.at[p], xin_ref.at[pl.ds(0, MOE_R)], sem.at[p]).wait()

    @pl.when(i == 0)
    def _():
        xs_ref[0, 0:MOE_XB, :] = jnp.zeros((MOE_XB, D_MODEL), F32)

        def zero_copy(d):
            return pltpu.make_async_copy(xs_ref.at[0, pl.ds(0, MOE_XB)],
                                         xin_ref.at[pl.ds(pl.multiple_of(d, MOE_XB), MOE_XB)], sem.at[0])

        def per_expert(fn):
            def body(e, c):
                d = lb_ref[e]

                @pl.when(d >= 0)
                def _():
                    fn(zero_copy(d))
                return c
            lax.fori_loop(0, N_EXPERTS, body, 0)

        def per_tail(fn):
            def body(b, c):
                fn(zero_copy(b * MOE_XB))
                return c
            lax.fori_loop(nv_ref[0], n_blocks, body, 0)

        per_expert(lambda cp: cp.start())
        per_tail(lambda cp: cp.start())
        per_expert(lambda cp: cp.wait())
        per_tail(lambda cp: cp.wait())

    @pl.when(i >= 2)
    def _():
        wait_tile(par)

    rt_t = lax.bitcast_convert_type(lax.bitcast_convert_type(rt_ref[...], F32).T, I32)
    expert = lax.broadcasted_iota(I32, (LANES, MOE_TM), 0)
    ls_row = ls_ref[0].astype(F32)
    ld = []
    for k in range(TOP_K):
        onehot = jnp.where(expert == rt_t[RT_ID + k:RT_ID + k + 1, :], 1.0, 0.0).astype(BF16)
        base = _dot_x2(ls_row, onehot)
        ld.append(base.astype(I32) + rt_t[RT_RANK + k:RT_RANK + k + 1, :])
    h2 = h2_ref[...]
    for c in range(MOE_R // MOE_RC):
        slot = c * MOE_RC + lax.broadcasted_iota(I32, (MOE_RC, MOE_TM), 0)
        hit = (slot == ld[0]) | (slot == ld[1]) | (slot == ld[2]) | (slot == ld[3])
        xs_ref[par, c * MOE_RC:(c + 1) * MOE_RC, :] = _dot(jnp.where(hit, 1.0, 0.0).astype(BF16), h2)

    def start(j, c):
        d = gd_ref[i * MOE_NG + j]
        _group_copy(xs_ref.at[par], xin_ref, sem.at[par], pl.multiple_of(j * SUBLANES, SUBLANES),
                    pl.multiple_of(d, SUBLANES)).start()
        return c

    lax.fori_loop(0, MOE_NG, start, 0, unroll=DMA_UNROLL)

    @pl.when(i == n_tiles - 1)
    def _():
        wait_tile(par)
        if n_tiles >= 2:
            wait_tile(1 - par)


def _dispatch(cfg, h2, rt, lstart, gdest, last_blk, nvalid, n_rows):
    tm = MOE_TM
    n = cfg.t // tm
    grid_spec = pltpu.PrefetchScalarGridSpec(
        num_scalar_prefetch=3,
        grid=(n,),
        in_specs=[pl.BlockSpec((tm, D_MODEL), lambda i, *_: (i, 0)),
                  pl.BlockSpec((tm, LANES), lambda i, *_: (i, 0)),
                  pl.BlockSpec((1, 1, LANES), lambda i, *_: (i, 0, 0))],
        out_specs=pl.BlockSpec(memory_space=pl.ANY),
        scratch_shapes=[pltpu.VMEM((2, MOE_R, D_MODEL), F32), pltpu.SemaphoreType.DMA((2,))],
    )
    return pl.pallas_call(
        functools.partial(_dispatch_kernel, n, n_rows // MOE_XB),
        grid_spec=grid_spec,
        out_shape=jax.ShapeDtypeStruct((n_rows, D_MODEL), F32),
        compiler_params=_cparams(),
        name="moe_dispatch",
    )(gdest, last_blk, nvalid, h2, rt, lstart)


def _expert_kernel(layer, be_ref, nv_ref, x_ref, wg_ref, bg_ref, wu_ref, bu_ref, wd_ref, bd_ref,
                   y_ref, wgc, wuc, wdc):
    del layer
    b = pl.program_id(0)
    prev = be_ref[jnp.maximum(b - 1, 0)]

    @pl.when((b == 0) | (be_ref[b] != prev))
    def _():
        wgc[...] = wg_ref[...].astype(BF16)
        wuc[...] = wu_ref[...].astype(BF16)
        wdc[...] = wd_ref[...].astype(BF16)

    @pl.when(b < nv_ref[0])
    def _():
        xb = x_ref[...].astype(BF16)
        gt = jnp.minimum(_dot(xb, wgc[...]) + bg_ref[...], SWIGLU_LIMIT)
        up = jnp.clip(_dot(xb, wuc[...]) + bu_ref[...], -SWIGLU_LIMIT, SWIGLU_LIMIT)
        hid = (up + 1.0) * gt * _sigmoid(SWIGLU_ALPHA * gt)
        y_ref[...] = _dot(hid.astype(BF16), wdc[...]) + bd_ref[...]

    @pl.when(b >= nv_ref[0])
    def _():
        y_ref[...] = jnp.zeros_like(y_ref)


def _experts(layer, xin, block_exp, nvalid, w_gate, b_gate, w_up, b_up, w_down, b_down):
    n_blocks = xin.shape[0] // MOE_XB
    d = D_MODEL
    nl, ne = w_gate.shape[0], w_gate.shape[1]
    xmap = lambda b, be, nv: (jnp.minimum(b, nv[0] - 1), 0)
    wmap = lambda b, be, nv: (layer, be[b], 0, 0)
    wspec = pl.BlockSpec((None, None, d, d), wmap)
    bspec = pl.BlockSpec((None, None, 1, d), wmap)
    grid_spec = pltpu.PrefetchScalarGridSpec(
        num_scalar_prefetch=2,
        grid=(n_blocks,),
        in_specs=[pl.BlockSpec((MOE_XB, d), xmap), wspec, bspec, wspec, bspec, wspec, bspec],
        out_specs=pl.BlockSpec((MOE_XB, d), lambda b, be, nv: (b, 0)),
        scratch_shapes=[pltpu.VMEM((d, d), BF16)] * 3,
    )
    return pl.pallas_call(
        functools.partial(_expert_kernel, layer),
        grid_spec=grid_spec,
        out_shape=jax.ShapeDtypeStruct(xin.shape, F32),
        compiler_params=_cparams(),
        name="moe_experts",
    )(block_exp, nvalid, xin, w_gate, b_gate.reshape(nl, ne, 1, d), w_up, b_up.reshape(nl, ne, 1, d),
      w_down, b_down.reshape(nl, ne, 1, d))


def _combine_kernel(cfg, final, gd_ref, rt_ref, ls_ref, x1_ref, pa_ref, pb_ref, pg_ref, pp_ref,
                    pn_ref, fn_ref, y_ref, *refs):
    o_refs, (ys_ref, sem) = refs[:-2], refs[-2:]
    n_tiles = cfg.t // MOE_TM
    i = pl.program_id(0)
    par = i % 2

    def fetch(tile, p):
        def start(j, c):
            d = gd_ref[tile * MOE_NG + j]
            _group_copy(y_ref, ys_ref.at[p], sem.at[p], pl.multiple_of(d, SUBLANES),
                        pl.multiple_of(j * SUBLANES, SUBLANES)).start()
            return c
        lax.fori_loop(0, MOE_NG, start, 0, unroll=DMA_UNROLL)

    @pl.when(i == 0)
    def _():
        fetch(0, 0)

    @pl.when(i + 1 < n_tiles)
    def _():
        fetch(i + 1, 1 - par)

    rt = rt_ref[...]
    ld = _local_dest(rt, ls_ref[0])
    gates = [lax.bitcast_convert_type(rt[:, RT_GATE + k:RT_GATE + k + 1], F32) for k in range(TOP_K)]
    pltpu.make_async_copy(y_ref.at[pl.ds(0, MOE_R)], ys_ref.at[par], sem.at[par]).wait()
    x2 = x1_ref[...]
    for c in range(MOE_R // MOE_RC):
        slot = c * MOE_RC + lax.broadcasted_iota(I32, (MOE_TM, MOE_RC), 1)
        ptg = jnp.zeros((MOE_TM, MOE_RC), F32)
        for k in range(TOP_K):
            ptg = jnp.where(slot == ld[k], gates[k], ptg)
        x2 = x2 + _dot(ptg.astype(BF16), ys_ref[par, c * MOE_RC:(c + 1) * MOE_RC, :].astype(BF16))
    gate = _sigmoid(_dot(x2.astype(BF16), pg_ref[...]))
    p = _group_load(cfg, MOE_TM, (pa_ref, pb_ref))
    e = _dot(p.astype(BF16), pp_ref[...]) * gate
    x3 = x2 + _rms(e, pn_ref[...])
    if not final:
        o_refs[0][...] = x3
    else:
        x3 = _rms(x3, fn_ref[...])
        is_prompt = i < cfg.t0 // MOE_TM

        @pl.when(is_prompt)
        def _():
            o_refs[0][...] = x3

        @pl.when(jnp.logical_not(is_prompt))
        def _():
            o_refs[1][...] = x3


def _combine(cfg, layer, final, rt, lstart, gdest, x1, p_prompt, p_sample, ple_gate, ple_proj,
             ple_norm, final_norm, y):
    tm = MOE_TM
    n = cfg.t // tm
    row = lambda w: pl.BlockSpec((tm, w), lambda i, gd: (i, 0))
    const = lambda *s: pl.BlockSpec(s, lambda i, gd: (0,) * len(s))
    if final:
        out_specs = _group_specs(cfg, tm, D_MODEL, 2)
        out_shape = [jax.ShapeDtypeStruct((cfg.t0, D_MODEL), F32),
                     jax.ShapeDtypeStruct((cfg.t1, D_MODEL), F32)]
    else:
        out_specs = [row(D_MODEL)]
        out_shape = [jax.ShapeDtypeStruct((cfg.t, D_MODEL), F32)]
    grid_spec = pltpu.PrefetchScalarGridSpec(
        num_scalar_prefetch=1,
        grid=(n,),
        in_specs=[row(LANES), pl.BlockSpec((1, 1, LANES), lambda i, gd: (i, 0, 0)), row(D_MODEL)]
                 + _group_specs(cfg, tm, PLE_DIM, 2, lead=layer)
                 + [const(D_MODEL, D_MODEL), const(PLE_DIM, D_MODEL),
                    const(1, D_MODEL), const(1, D_MODEL), pl.BlockSpec(memory_space=pl.ANY)],
        out_specs=out_specs,
        scratch_shapes=[pltpu.VMEM((2, MOE_R, D_MODEL), F32), pltpu.SemaphoreType.DMA((2,))],
    )
    outs = pl.pallas_call(
        functools.partial(_combine_kernel, cfg, final),
        grid_spec=grid_spec,
        out_shape=out_shape,
        compiler_params=_cparams(),
        name="moe_combine",
    )(gdest, rt, lstart, x1, p_prompt, p_sample, ple_gate.astype(BF16), ple_proj.astype(BF16),
      ple_norm.reshape(1, D_MODEL), final_norm.reshape(1, D_MODEL), y)
    return outs if final else outs[0]


def _layer(cfg, layer, final, x, p_prompt, p_sample, tabs, norm1, w_in, conv_w, dn_a_log,
           dn_dt_bias, dn_norm, ret_decay, ret_norm, w_out, norm2, router_w, router_b, w_gate,
           b_gate, w_up, b_up, w_down, b_down, ple_proj, ple_gate, ple_norm, final_norm):
    proj = _in_proj(cfg, x, norm1[layer], _reorder_w_in(w_in[layer]), tabs[0], tabs[1], PROJ_TM)
    qkv, gates = _dn_prep(cfg, proj, conv_w[layer], dn_a_log[layer], dn_dt_bias[layer], PROJ_TM)
    dn_f, dn_b = _dn_scan(cfg, qkv, gates, SCAN_TM)
    ret_f, ret_b = _ret_scan(cfg, proj, _ret_tables(ret_decay[layer]), SCAN_TM)
    att = _attention(cfg, proj)
    x1, h2, rt, cnt = _post(cfg, dn_f, dn_b, ret_f, ret_b, att, proj, x, w_out[layer],
                            dn_norm[layer], ret_norm[layer], norm2[layer], router_w[layer],
                            router_b[layer])
    n_rows = _moe_rows(cfg)
    lstart, gdest, block_exp, last_blk, nvalid = _route_tables(cnt[:, 0, :], n_rows)
    xin = _dispatch(cfg, h2, rt, lstart, gdest, last_blk, nvalid, n_rows)
    y = _experts(layer, xin, block_exp, nvalid, w_gate, b_gate, w_up, b_up, w_down, b_down)
    return _combine(cfg, layer, final, rt, lstart, gdest, x1, p_prompt, p_sample, ple_gate[layer],
                    ple_proj[layer], ple_norm[layer], final_norm, y)


def _trunk(cfg, x_parts, p_prompt, p_sample, final_norm, params):
    tabs = _rope_tables(cfg.smax)
    depth = p_prompt.shape[0]
    x = x_parts
    for layer in range(depth):
        final = layer == depth - 1
        out = _layer(cfg, layer, final, x, p_prompt, p_sample, tabs, *params, final_norm)
        x = out if final else (out,)
    return x


def kernel(x_prompt, x_sample, p_prompt, p_sample, norm1, w_in, conv_w, dn_a_log, dn_dt_bias,
           dn_norm, ret_decay, ret_norm, w_out, norm2, router_w, router_b, w_gate, b_gate,
           w_up, b_up, w_down, b_down, ple_proj, ple_gate, ple_norm, final_norm):
    b0, s0, d = x_prompt.shape
    b1, s1, _ = x_sample.shape
    cfg = Cfg(b0, s0, b1, s1)
    depth = p_prompt.shape[0]
    params = (norm1, w_in, conv_w, dn_a_log, dn_dt_bias, dn_norm, ret_decay, ret_norm, w_out,
              norm2, router_w, router_b, w_gate, b_gate, w_up, b_up, w_down, b_down,
              ple_proj, ple_gate, ple_norm)
    y0, y1 = _trunk(cfg, (x_prompt.reshape(cfg.t0, d), x_sample.reshape(cfg.t1, d)),
                    p_prompt.reshape(depth, cfg.t0, PLE_DIM), p_sample.reshape(depth, cfg.t1, PLE_DIM),
                    final_norm, params)
    return (y0.reshape(b0, s0, d), y1.reshape(b1, s1, d))
```

```python
import functools
import math

import jax
import jax.numpy as jnp
import numpy as np
from jax import lax
from jax.experimental import pallas as pl
from jax.experimental.pallas import tpu as pltpu

F32 = jnp.float32
BF16 = jnp.bfloat16
I32 = jnp.int32

D_MODEL = 1024
HEAD_DIM = 64
DN_HEADS = 4
RET_HEADS = 4
ATT_HEADS = 8
DN_WIDTH = 256
RET_WIDTH = 256
ATT_WIDTH = 512
CONV_W = 4
RET_THETA = 10000.0
ROPE_THETA = 500000.0
ROPE_DIM = HEAD_DIM // 4
DILATED_PATTERNS = ((128, 1), (512, 4), (2048, 16))
ATT_HALF = 64
N_EXPERTS = 32
TOP_K = 4
SWIGLU_ALPHA = 1.702
SWIGLU_LIMIT = 7.0
PLE_DIM = 256
NORM_EPS = 1e-6
NEG_BIG = -1e30

LANES = 128
SUBLANES = 8
VMEM_LIMIT = 56 * 1024 * 1024

COL_DN_QKV = 0
COL_DN_Z = 768
COL_RQ = 1024
COL_RK = 1280
COL_RV = 1536
COL_RG = 1792
COL_AQ = 2048
COL_AK = 2560
COL_AV = 3072
COL_AB = 3584
PROJ_W = 3840

PROJ_TM = 512
SCAN_TM = 256
CHUNK = 64
CAT = 4 * CHUNK


class Cfg:
    def __init__(self, b0, s0, b1, s1):
        self.b0, self.s0, self.b1, self.s1 = b0, s0, b1, s1
        self.t0 = b0 * s0
        self.t1 = b1 * s1
        self.t = self.t0 + self.t1
        self.smax = max(s0, s1)

    def pos(self, row):
        return jnp.where(row < self.t0, row % self.s0, (row - self.t0) % self.s1)

    def seq_len(self, row):
        return jnp.where(row < self.t0, self.s0, self.s1)


def _cparams(n_axes=1, sem=None):
    return pltpu.CompilerParams(
        dimension_semantics=tuple(sem or ("arbitrary",) * n_axes),
        vmem_limit_bytes=VMEM_LIMIT)


def _group_specs(cfg, tm, width, n_parts, lead=None):
    pre_b = () if lead is None else (None,)
    pre_i = () if lead is None else (lead,)
    if n_parts == 1:
        return [pl.BlockSpec(pre_b + (tm, width), lambda i, *_: pre_i + (i, 0))]
    n0 = cfg.t0 // tm
    return [pl.BlockSpec(pre_b + (tm, width), lambda i, *_: pre_i + (jnp.minimum(i, n0 - 1), 0)),
            pl.BlockSpec(pre_b + (tm, width), lambda i, *_: pre_i + (jnp.maximum(i - n0, 0), 0))]


def _group_load(cfg, tm, refs, rows=slice(None)):
    if len(refs) == 1:
        return refs[0][rows, :]
    return jnp.where(pl.program_id(0) < cfg.t0 // tm, refs[0][rows, :], refs[1][rows, :])


def _split_bf16(x):
    hi = x.astype(BF16)
    lo = (x - hi.astype(F32)).astype(BF16)
    return hi, lo


def _dot(a, b):
    return jnp.dot(a, b, preferred_element_type=F32)


def _dot_x2(a, b_exact):
    hi, lo = _split_bf16(a)
    return _dot(hi, b_exact) + _dot(lo, b_exact)


def _dot_lx2(a_exact, b):
    hi, lo = _split_bf16(b)
    return _dot(a_exact, hi) + _dot(a_exact, lo)


def _dot_x3(a, b):
    ah, al = _split_bf16(a)
    bh, bl = _split_bf16(b)
    return _dot(ah, bh) + (_dot(al, bh) + _dot(ah, bl))


def _sigmoid(x):
    return 0.5 * jnp.tanh(0.5 * x) + 0.5


def _silu(x):
    return x * _sigmoid(x)


def _head_sumsq(x, hsum):
    return _dot((x * x).astype(BF16), hsum)


def _head_sum_matrix(width):
    r = lax.broadcasted_iota(I32, (width, width), 0) // HEAD_DIM
    c = lax.broadcasted_iota(I32, (width, width), 1) // HEAD_DIM
    return jnp.where(r == c, 1.0, 0.0).astype(BF16)


_SECTIONS = tuple((c, 1 if c in (COL_RQ, COL_RK) else 2 if COL_AQ <= c < COL_AV else 0)
                  for c in range(0, PROJ_W, 256))


def _rope_tables(smax):
    pos = jnp.arange(smax, dtype=F32)[:, None]
    d = jnp.arange(256) % HEAD_DIM

    def build(inv, half):
        rot = 2 * half
        ang = pos * inv[d % half][None, :]
        cos, sin = jnp.cos(ang), jnp.sin(ang)
        in_rot = (d < rot)[None, :]
        first = (d < half)[None, :]
        c = jnp.where(in_rot, cos, 1.0)
        s_plus = jnp.where(in_rot & ~first, sin, 0.0)
        s_minus = jnp.where(first, -sin, 0.0)
        return jnp.stack([c, s_plus, s_minus])

    inv_ret = 1.0 / jnp.power(jnp.float32(RET_THETA), jnp.linspace(0.0, 1.0, HEAD_DIM // 2, dtype=F32))
    inv_att = jnp.power(jnp.float32(ROPE_THETA), -jnp.arange(0, ROPE_DIM, 2, dtype=F32) / ROPE_DIM)
    return build(inv_ret, HEAD_DIM // 2), build(inv_att, ROPE_DIM // 2)


def _in_proj_kernel(cfg, tm, nx, *refs):
    g_ref, w_ref, tr_ref, ta_ref, o_ref = refs[nx:]
    x = _group_load(cfg, tm, refs[:nx])
    y = x * lax.rsqrt(jnp.mean(x * x, axis=-1, keepdims=True) + NORM_EPS)
    h = (y * g_ref[...]).astype(BF16)
    for col, kind in _SECTIONS:
        acc = _dot(h, w_ref[:, col:col + 256])
        if kind:
            t_ref, half = (tr_ref, HEAD_DIM // 2) if kind == 1 else (ta_ref, ROPE_DIM // 2)
            acc = (acc * t_ref[0] + pltpu.roll(acc, half, 1) * t_ref[1]
                   + pltpu.roll(acc, 256 - half, 1) * t_ref[2])
        o_ref[:, col:col + 256] = acc


def _in_proj(cfg, xs, g1, w_bf16, tab_ret, tab_att, tm):
    n = cfg.t // tm

    def tab_map(i):
        return (0, cfg.pos(i * tm) // tm, 0)

    return pl.pallas_call(
        functools.partial(_in_proj_kernel, cfg, tm, len(xs)),
        grid=(n,),
        in_specs=_group_specs(cfg, tm, D_MODEL, len(xs)) + [
            pl.BlockSpec((1, D_MODEL), lambda i: (0, 0)),
            pl.BlockSpec((D_MODEL, PROJ_W), lambda i: (0, 0)),
            pl.BlockSpec((3, tm, 256), tab_map),
            pl.BlockSpec((3, tm, 256), tab_map),
        ],
        out_specs=pl.BlockSpec((tm, PROJ_W), lambda i: (i, 0)),
        out_shape=jax.ShapeDtypeStruct((cfg.t, PROJ_W), F32),
        compiler_params=_cparams(),
        name="in_proj",
    )(*xs, g1.reshape(1, D_MODEL), w_bf16, tab_ret, tab_att)


def _reorder_w_in(w_in):
    ab = w_in[:, 1024:1040]
    rest = w_in[:, 1040:]
    pad = jnp.zeros((w_in.shape[0], PROJ_W - 3600), w_in.dtype)
    return jnp.concatenate([w_in[:, :1024], rest, ab, pad], axis=1).astype(BF16)


def _dn_prep_kernel(cfg, tm, cur_ref, prev_ref, next_ref, ab_ref, cw_ref, ga_ref, gb_ref,
                    qkv_ref, gate_ref, ext_ref):
    i = pl.program_id(0)
    pos = cfg.pos(i * tm)
    first = pos == 0
    last = pos + tm == cfg.seq_len(i * tm)
    ext_ref[0:8, :] = jnp.where(first, 0.0, prev_ref[...])
    ext_ref[8:8 + tm, :] = cur_ref[...]
    ext_ref[8 + tm:16 + tm, :] = jnp.where(last, 0.0, next_ref[...])
    y = cw_ref[0:1, :] * ext_ref[6:6 + tm, :]
    for j in range(1, CONV_W):
        y = y + cw_ref[j:j + 1, :] * ext_ref[6 + j:6 + j + tm, :]
    y = _silu(y)
    hsum = _head_sum_matrix(DN_WIDTH)
    q = y[:, 0:DN_WIDTH]
    k = y[:, DN_WIDTH:2 * DN_WIDTH]
    qkv_ref[:, 0:DN_WIDTH] = q * lax.rsqrt(_head_sumsq(q, hsum) + NORM_EPS) * (HEAD_DIM ** -0.5)
    qkv_ref[:, DN_WIDTH:2 * DN_WIDTH] = k * lax.rsqrt(_head_sumsq(k, hsum) + NORM_EPS)
    qkv_ref[:, 2 * DN_WIDTH:] = y[:, 2 * DN_WIDTH:]
    ab = ab_ref[:, 0:LANES]
    xa = ab + gb_ref[...]
    softplus = jnp.maximum(xa, 0.0) + jnp.log(1.0 + jnp.exp(-jnp.abs(xa)))
    lane = lax.broadcasted_iota(I32, ab.shape, 1)
    gate_ref[...] = jnp.where(lane < 2 * DN_HEADS, ga_ref[...] * softplus, _sigmoid(ab))


def _dn_prep(cfg, proj, conv_w, a_log, dt_bias, tm):
    n = cfg.t // tm
    w3 = 3 * DN_WIDTH
    nh = tm // 8
    last8 = cfg.t // 8 - 1
    neg_a = jnp.zeros((1, LANES), F32).at[0, :2 * DN_HEADS].set(-jnp.exp(a_log.reshape(-1)))
    dtb = jnp.zeros((1, LANES), F32).at[0, :2 * DN_HEADS].set(dt_bias.reshape(-1))
    return pl.pallas_call(
        functools.partial(_dn_prep_kernel, cfg, tm),
        grid=(n,),
        in_specs=[
            pl.BlockSpec((tm, w3), lambda i: (i, 0)),
            pl.BlockSpec((8, w3), lambda i: (jnp.maximum(i * nh - 1, 0), 0)),
            pl.BlockSpec((8, w3), lambda i: (jnp.minimum((i + 1) * nh, last8), 0)),
            pl.BlockSpec((tm, 256), lambda i: (i, COL_AB // 256)),
            pl.BlockSpec((CONV_W, w3), lambda i: (0, 0)),
            pl.BlockSpec((1, LANES), lambda i: (0, 0)),
            pl.BlockSpec((1, LANES), lambda i: (0, 0)),
        ],
        out_specs=[
            pl.BlockSpec((tm, w3), lambda i: (i, 0)),
            pl.BlockSpec((tm, LANES), lambda i: (i, 0)),
        ],
        out_shape=[jax.ShapeDtypeStruct((cfg.t, w3), F32),
                   jax.ShapeDtypeStruct((cfg.t, LANES), F32)],
        scratch_shapes=[pltpu.VMEM((tm + 16, w3), F32)],
        compiler_params=_cparams(),
        name="dn_prep",
    )(proj, proj, proj, proj, conv_w, neg_a, dtb)


def _bd_mask():
    r = lax.broadcasted_iota(I32, (CAT, CAT), 0) // CHUNK
    c = lax.broadcasted_iota(I32, (CAT, CAT), 1) // CHUNK
    return r == c, jnp.where(r == c, 1.0, 0.0).astype(BF16)


def _tile4(x):
    xb = x.astype(BF16)
    return jnp.concatenate([xb, xb, xb, xb], axis=0)


def _bd(x, bdmask):
    return _tile4(x) * bdmask


def _dot_t0(a, b):
    return lax.dot_general(a, b, (((0,), (0,)), ((), ())), preferred_element_type=F32)


def _dot_t1(a, b):
    return lax.dot_general(a, b, (((1,), (1,)), ((), ())), preferred_element_type=F32)


def _chunk_ij(rows):
    i = lax.broadcasted_iota(I32, (rows, CAT), 0) % CHUNK
    j = lax.broadcasted_iota(I32, (rows, CAT), 1) % CHUNK
    return i, j


def _segments(cfg, tm, want):
    for nseg in (want, 2, 1):
        seg = cfg.t // nseg
        if cfg.t % nseg == 0 and seg % tm == 0 and all(
                (s * seg) % cfg.s0 == 0 if s * seg <= cfg.t0 else (s * seg - cfg.t0) % cfg.s1 == 0
                for s in range(nseg)):
            return nseg
    return 1


def _stream_rows(nseg, per, tm):
    fns = []
    for s in range(nseg):
        fns.append(lambda i, s=s: (s * per + i) * tm)
        fns.append(lambda i, s=s: (s * per + per - 1 - i) * tm)
    return fns


def _stream_views(nseg, fwd_ref, bwd_ref):
    out = []
    for s in range(nseg):
        out += [fwd_ref.at[s], bwd_ref.at[s]]
    return out


def _dn_streams(revs, qkv_refs, gate_refs, o_refs, s_refs, tm):
    nc = tm // CHUNK
    bd_bool, bdmask = _bd_mask()
    i, j = _chunk_ij(tm)
    ic, jc = _chunk_ij(CHUNK)
    eye = jnp.where(ic == jc, 1.0, 0.0)
    lvl_masks = {lvl: _bd(jnp.where(((ic >> lvl) == (jc >> lvl)) & ((ic >> (lvl - 1)) != (jc >> (lvl - 1))),
                                    1.0, 0.0), bdmask) for lvl in range(2, 7)}
    l = lax.broadcasted_iota(I32, (LANES, 2 * CAT), 0)
    c = lax.broadcasted_iota(I32, (LANES, 2 * CAT), 1)
    ti = lax.broadcasted_iota(I32, (tm, tm), 0)
    tk = lax.broadcasted_iota(I32, (tm, tm), 1)
    same = (ti // CHUNK) == (tk // CHUNK)
    consts = {}
    for rev in sorted(set(revs)):
        base = (4 if rev else 0) + jnp.where(c >= CAT, 2 * DN_HEADS, 0)
        consts[rev] = dict(
            sel=jnp.where(l == base + (c % CAT) // CHUNK, 1.0, 0.0).astype(BF16),
            tri=jnp.where(same & ((tk >= ti) if rev else (tk <= ti)), 1.0, 0.0).astype(BF16),
            between=(i < j) if rev else (i > j),
            incl=(i <= j) if rev else (i >= j),
            strict=(ic < jc) if rev else (ic > jc))

    st = []
    for rev, qkv_ref, gate_ref in zip(revs, qkv_refs, gate_refs):
        cs = consts[rev]
        gb = _dot_x2(gate_ref[...], cs["sel"])
        g_cat, beta = gb[:, :CAT], gb[:, CAT:]
        gc = _dot_lx2(cs["tri"], g_cat)
        diff = _dot_lx2(cs["tri"], jnp.where(cs["between"], g_cat, 0.0))
        st.append(dict(rev=rev, beta=beta, gc=gc, egc=jnp.exp(gc),
                       decay=jnp.where(cs["incl"], jnp.exp(diff), 0.0),
                       q=qkv_ref[:, 0:CAT], k=qkv_ref[:, CAT:2 * CAT], v=qkv_ref[:, 2 * CAT:3 * CAT]))

    chains = [(si, ci) for si in range(len(st)) for ci in range(nc)]
    attn, amat, tinv = {}, {}, {}
    for si, ci in chains:
        s_, r = st[si], slice(ci * CHUNK, (ci + 1) * CHUNK)
        qk_kk = _dot_t1(jnp.concatenate([s_["q"][r], s_["k"][r]], axis=0).astype(BF16),
                        _bd(s_["k"][r], bdmask))
        attn[si, ci] = qk_kk[:CHUNK] * s_["decay"][r]
        a = jnp.where(consts[s_["rev"]]["strict"], qk_kk[CHUNK:] * s_["beta"][r] * s_["decay"][r], 0.0)
        amat[si, ci] = _tile4(a)
        tinv[si, ci] = eye - jnp.where((ic >> 1) == (jc >> 1), a, 0.0)
    for lvl in range(2, 7):
        xs = {key: _dot(tinv[key].astype(BF16), amat[key] * lvl_masks[lvl]) for key in chains}
        for key in chains:
            tinv[key] = tinv[key] - _dot(xs[key].astype(BF16), _bd(tinv[key], bdmask))

    states = [s_ref[...] for s_ref in s_refs]
    for step in range(nc):
        idx = [(nc - 1 - step) if s_["rev"] else step for s_ in st]
        rs = [slice(ci * CHUNK, (ci + 1) * CHUNK) for ci in idx]
        x1s = [_dot(jnp.concatenate([s_["k"][r] * s_["beta"][r] * s_["egc"][r], s_["q"][r] * s_["egc"][r]],
                                    axis=0).astype(BF16), state.astype(BF16))
               for s_, r, state in zip(st, rs, states)]
        vns = [_dot(tinv[si, ci].astype(BF16), _bd(s_["v"][r] * s_["beta"][r] - x1[:CHUNK], bdmask))
               for si, (s_, r, ci, x1) in enumerate(zip(st, rs, idx, x1s))]
        for si, (s_, r, ci, x1, vn, o_ref) in enumerate(zip(st, rs, idx, x1s, vns, o_refs)):
            o_ref[r, :] = x1[CHUNK:] + _dot(attn[si, ci].astype(BF16), _bd(vn, bdmask))
            gcc = s_["gc"][r]
            g_last = gcc[0:1] if s_["rev"] else gcc[CHUNK - 1:CHUNK]
            k_dec = s_["k"][r] * jnp.exp(g_last - gcc)
            states[si] = states[si] * jnp.exp(g_last) + jnp.where(
                bd_bool, _dot_t0(k_dec.astype(BF16), vn.astype(BF16)), 0.0)
    for s_ref, state in zip(s_refs, states):
        s_ref[...] = state


def _reset_states(cfg, tm, row_fns, s_refs):
    i = pl.program_id(0)
    for si, (row_fn, s_ref) in enumerate(zip(row_fns, s_refs)):
        row = row_fn(i)
        edge = (cfg.pos(row) + tm == cfg.seq_len(row)) if si % 2 else (cfg.pos(row) == 0)

        @pl.when(edge)
        def _(s_ref=s_ref):
            s_ref[...] = jnp.zeros_like(s_ref)


def _dn_scan_kernel(cfg, tm, nseg, per, qf_ref, gf_ref, qb_ref, gb_ref, of_ref, ob_ref, *s_refs):
    _reset_states(cfg, tm, _stream_rows(nseg, per, tm), s_refs)
    _dn_streams([si % 2 == 1 for si in range(2 * nseg)], _stream_views(nseg, qf_ref, qb_ref),
                _stream_views(nseg, gf_ref, gb_ref), _stream_views(nseg, of_ref, ob_ref), s_refs, tm)


def _scan_specs(nseg, per, tm, width, col_block=0):
    return (pl.BlockSpec((nseg, tm, width), lambda i: (0, i, col_block)),
            pl.BlockSpec((nseg, tm, width), lambda i: (0, per - 1 - i, col_block)))


def _dn_scan(cfg, qkv, gate, tm, want_segments=2):
    nseg = _segments(cfg, tm, want_segments)
    per = cfg.t // nseg // tm
    w3 = 3 * DN_WIDTH
    qf, qb = _scan_specs(nseg, per, tm, w3)
    gf, gb = _scan_specs(nseg, per, tm, LANES)
    of, ob = _scan_specs(nseg, per, tm, DN_WIDTH)
    qkv3 = qkv.reshape(nseg, cfg.t // nseg, w3)
    gate3 = gate.reshape(nseg, cfg.t // nseg, LANES)
    o_f, o_b = pl.pallas_call(
        functools.partial(_dn_scan_kernel, cfg, tm, nseg, per),
        grid=(per,),
        in_specs=[qf, gf, qb, gb],
        out_specs=[of, ob],
        out_shape=[jax.ShapeDtypeStruct((nseg, cfg.t // nseg, DN_WIDTH), F32)] * 2,
        scratch_shapes=[pltpu.VMEM((CAT, CAT), F32)] * (2 * nseg),
        compiler_params=_cparams(),
        name="dn_scan",
    )(qkv3, gate3, qkv3, gate3)
    return o_f.reshape(cfg.t, DN_WIDTH), o_b.reshape(cfg.t, DN_WIDTH)


def _ret_tables(decay_logit):
    lg = jnp.log(jax.nn.sigmoid(decay_logit.astype(F32)))
    i = jnp.arange(CHUNK, dtype=F32)[:, None]
    j = (jnp.arange(CAT) % CHUNK).astype(F32)[None, :]
    scale = HEAD_DIM ** -0.5
    out = []
    for d in range(2):
        lgl = jnp.repeat(lg[d], CHUNK)[None, :]
        rel = (j - i) if d else (i - j)
        dmat = jnp.where(rel >= 0, jnp.exp(jnp.where(rel >= 0, rel, 0.0) * lgl), 0.0) * scale
        qdec = jnp.exp(((CHUNK - i) if d else (i + 1.0)) * lgl)
        kdec = jnp.exp((i if d else (CHUNK - 1.0 - i)) * lgl) * scale
        cd = jnp.broadcast_to(jnp.exp(CHUNK * lgl), (CHUNK, CAT))
        out.append(jnp.stack([dmat, qdec, kdec, cd]))
    return jnp.stack(out)


def _ret_streams(revs, q_refs, k_refs, v_refs, tab_ref, o_refs, s_refs, tm):
    nc = tm // CHUNK
    bd_bool, bdmask = _bd_mask()
    tabs = {rev: [tab_ref[1 if rev else 0, t] for t in range(4)] for rev in sorted(set(revs))}
    inner = {}
    for si, rev in enumerate(revs):
        dmat = tabs[rev][0]
        for ci in range(nc):
            r = slice(ci * CHUNK, (ci + 1) * CHUNK)
            scores = _dot_t1(q_refs[si][r, :].astype(BF16), _bd(k_refs[si][r, :], bdmask)) * dmat
            inner[si, ci] = _dot(scores.astype(BF16), _bd(v_refs[si][r, :], bdmask))
    states = [s_ref[...] for s_ref in s_refs]
    for step in range(nc):
        for si, rev in enumerate(revs):
            _, qdec, kdec, cd = tabs[rev]
            ci = (nc - 1 - step) if rev else step
            r = slice(ci * CHUNK, (ci + 1) * CHUNK)
            qc, kc, vc = q_refs[si][r, :], k_refs[si][r, :], v_refs[si][r, :]
            o_refs[si][r, :] = inner[si, ci] + _dot((qc * qdec).astype(BF16), states[si].astype(BF16))
            kv = _dot_t0((kc * kdec).astype(BF16), vc.astype(BF16))
            states[si] = states[si] * cd[0:1] + jnp.where(bd_bool, kv, 0.0)
    for s_ref, state in zip(s_refs, states):
        s_ref[...] = state


def _ret_scan_kernel(cfg, tm, nseg, per, qf, kf, vf, qb, kb, vb, tab_ref, of_ref, ob_ref, *s_refs):
    _reset_states(cfg, tm, _stream_rows(nseg, per, tm), s_refs)
    _ret_streams([si % 2 == 1 for si in range(2 * nseg)], _stream_views(nseg, qf, qb),
                 _stream_views(nseg, kf, kb), _stream_views(nseg, vf, vb), tab_ref,
                 _stream_views(nseg, of_ref, ob_ref), s_refs, tm)


def _ret_scan(cfg, proj, tables, tm, want_segments=2):
    nseg = _segments(cfg, tm, want_segments)
    per = cfg.t // nseg // tm
    proj3 = proj.reshape(nseg, cfg.t // nseg, PROJ_W)
    specs = [_scan_specs(nseg, per, tm, 256, col // 256) for col in (COL_RQ, COL_RK, COL_RV)]
    of, ob = _scan_specs(nseg, per, tm, RET_WIDTH)
    o_f, o_b = pl.pallas_call(
        functools.partial(_ret_scan_kernel, cfg, tm, nseg, per),
        grid=(per,),
        in_specs=[s[0] for s in specs] + [s[1] for s in specs]
                 + [pl.BlockSpec((2, 4, CHUNK, CAT), lambda i: (0, 0, 0, 0))],
        out_specs=[of, ob],
        out_shape=[jax.ShapeDtypeStruct((nseg, cfg.t // nseg, RET_WIDTH), F32)] * 2,
        scratch_shapes=[pltpu.VMEM((CAT, CAT), F32)] * (2 * nseg),
        compiler_params=_cparams(),
        name="ret_scan",
    )(proj3, proj3, proj3, proj3, proj3, proj3, tables)
    return o_f.reshape(cfg.t, RET_WIDTH), o_b.reshape(cfg.t, RET_WIDTH)


ATT_TQ = 2048
ATT_HALO = 1024
ATT_QB = 128


def _ds(start, size, stride):
    return pl.ds(start, size) if stride == 1 else pl.ds(start, size, stride=stride)


ATT_NB = {1: 8, 4: 8, 16: 4}


def _att_kernel(cfg, q_ref, kp_ref, kc_ref, kn_ref, vp_ref, vc_ref, vn_ref, o_ref,
                kbuf, vbuf, bias_sc, m_sc, l_sc, acc_sc):
    tq, halo, qb = ATT_TQ, ATT_HALO, ATT_QB
    kw = qb + 2 * ATT_HALF
    i = pl.program_id(1)
    pos0 = cfg.pos(i * tq)
    slen = cfg.seq_len(i * tq)
    kbuf[0:halo, :] = kp_ref[...]
    kbuf[halo:halo + tq, :] = kc_ref[...]
    kbuf[halo + tq:, :] = kn_ref[...]
    vbuf[0:halo, :] = vp_ref[...]
    vbuf[halo:halo + tq, :] = vc_ref[...]
    vbuf[halo + tq:, :] = vn_ref[...]
    head0 = lax.broadcasted_iota(I32, (qb, LANES), 1) < HEAD_DIM
    qi = lax.broadcasted_iota(I32, (qb, kw), 0)
    kj = lax.broadcasted_iota(I32, (qb, kw), 1)
    bias_sc[...] = jnp.where((kj - qi >= 0) & (kj - qi <= 2 * ATT_HALF), 0.0, NEG_BIG)
    krow = lax.broadcasted_iota(I32, (1, kw), 1)
    qscale = HEAD_DIM ** -0.5 * math.log2(math.e)
    ones_v = jnp.ones((kw, LANES), BF16)

    for p, (_, dil) in enumerate(DILATED_PATTERNS):
        nb = ATT_NB[dil]

        def body(it, carry, p=p, dil=dil, nb=nb):
            where_q, where_k, scores = [], [], []
            for jj in range(nb):
                u = it * nb + jj
                r = u % dil
                b = u // dil
                qs = r + dil * qb * b
                ks = halo + r + dil * (qb * b - ATT_HALF)
                where_q.append(qs)
                where_k.append(ks)
                q = q_ref[_ds(qs, qb, dil), :] * qscale
                k = kbuf[_ds(ks, kw, dil), :].astype(BF16)
                kpos = pos0 // dil + qb * b - ATT_HALF + krow
                valid = jnp.where((kpos >= 0) & (kpos < slen // dil), 0.0, NEG_BIG)
                for h in range(2):
                    qh = jnp.where(head0 if h == 0 else ~head0, q, 0.0).astype(BF16)
                    scores.append(_dot_t1(qh, k) + bias_sc[...] + valid)
            ms = [jnp.max(s, axis=-1, keepdims=True) for s in scores]
            ps = [jnp.exp2(s - m).astype(BF16) for s, m in zip(scores, ms)]
            for jj in range(nb):
                v = vbuf[_ds(where_k[jj], kw, dil), :].astype(BF16)
                vext = jnp.concatenate([v, ones_v], axis=1)
                a0 = _dot(ps[2 * jj], vext)
                a1 = _dot(ps[2 * jj + 1], vext)
                rows = _ds(where_q[jj], qb, dil)
                m_sc[p, rows, :] = jnp.where(head0, ms[2 * jj], ms[2 * jj + 1])
                l_sc[p, rows, :] = jnp.where(head0, a0[:, LANES:], a1[:, LANES:])
                acc_sc[p, rows, :] = jnp.where(head0, a0[:, :LANES], a1[:, :LANES])
            return carry

        lax.fori_loop(0, tq // qb // nb, body, 0)

    m_all = [m_sc[p] for p in range(3)]
    mx = jnp.maximum(jnp.maximum(m_all[0], m_all[1]), m_all[2])
    num = jnp.zeros((tq, LANES), F32)
    den = jnp.zeros((tq, LANES), F32)
    for p in range(3):
        w = jnp.exp2(m_all[p] - mx)
        num = num + w * acc_sc[p]
        den = den + w * l_sc[p]
    o_ref[...] = num / den


def _attention(cfg, proj):
    tq, halo = ATT_TQ, ATT_HALO
    n = cfg.t // tq
    per = tq // halo
    nh = cfg.t // halo

    def cur(col):
        return pl.BlockSpec((tq, LANES), lambda hp, i: (i, col // LANES + hp))

    def prev(col):
        return pl.BlockSpec((halo, LANES), lambda hp, i: (jnp.maximum(i * per - 1, 0), col // LANES + hp))

    def nxt(col):
        return pl.BlockSpec((halo, LANES), lambda hp, i: (jnp.minimum((i + 1) * per, nh - 1), col // LANES + hp))

    return pl.pallas_call(
        functools.partial(_att_kernel, cfg),
        grid=(ATT_HEADS // 2, n),
        in_specs=[cur(COL_AQ), prev(COL_AK), cur(COL_AK), nxt(COL_AK),
                  prev(COL_AV), cur(COL_AV), nxt(COL_AV)],
        out_specs=pl.BlockSpec((tq, LANES), lambda hp, i: (i, hp)),
        out_shape=jax.ShapeDtypeStruct((cfg.t, ATT_WIDTH), F32),
        scratch_shapes=[pltpu.VMEM((tq + 2 * halo, LANES), F32)] * 2
                       + [pltpu.VMEM((ATT_QB, ATT_QB + 2 * ATT_HALF), F32)]
                       + [pltpu.VMEM((3, tq, LANES), F32)] * 3,
        compiler_params=_cparams(2),
        name="attention",
    )(proj, proj, proj, proj, proj, proj, proj)


MOE_TM = 512
MOE_R = 4 * MOE_TM + 256
MOE_NG = MOE_R // SUBLANES
MOE_RC = MOE_R // 3
MOE_XB = 512
MOE_DUMP = -(-MOE_R // MOE_XB) * MOE_XB
RT_ID, RT_RANK, RT_GATE = 0, 4, 8
POST_SUB = MOE_TM
DMA_UNROLL = 8


def _rms(x, g):
    return x * lax.rsqrt(jnp.mean(x * x, axis=-1, keepdims=True) + NORM_EPS) * g


def _post_kernel(cfg, nx, dnf, dnb, z_ref, rtf, rtb, rg_ref, att_ref, *refs):
    (wout_ref, dnn_ref, rtn_ref, n2_ref, rw_ref, rb_ref, ltri_ref,
     x1_ref, h2_ref, rt_ref, cnt_ref) = refs[nx:]
    hsum = _head_sum_matrix(DN_WIDTH)
    inv = 1.0 / HEAD_DIM
    tm, sub = MOE_TM, POST_SUB
    lane = lax.broadcasted_iota(I32, (sub, LANES), 1)
    lane_f = lane.astype(F32)
    parts = []
    for s in range(tm // sub):
        r = slice(s * sub, (s + 1) * sub)
        o = dnf[r, :] + dnb[r, :]
        dn = o * lax.rsqrt(_head_sumsq(o, hsum) * inv + NORM_EPS) * dnn_ref[...] * _silu(z_ref[r, :])
        o = rtf[r, :] + rtb[r, :]
        oc = o - _dot_x2(o, hsum) * inv
        ret = (oc * lax.rsqrt(_head_sumsq(oc, hsum) * inv + NORM_EPS) * rtn_ref[...]
               * _silu(rg_ref[r, :]))
        mixed = (_dot(dn.astype(BF16), wout_ref[0:256, :]) + _dot(ret.astype(BF16), wout_ref[256:512, :])
                 + _dot(att_ref[r, :].astype(BF16), wout_ref[512:1024, :]))
        x1 = _group_load(cfg, MOE_TM, refs[:nx], r) + mixed
        x1_ref[r, :] = x1
        h2 = _rms(x1, n2_ref[...])
        h2_ref[r, :] = h2.astype(BF16)
        hh, hl = _split_bf16(h2)
        work = _dot(hh, rw_ref[0]) + (_dot(hl, rw_ref[0]) + _dot(hh, rw_ref[1])) + rb_ref[...]
        vals, ids = [], []
        for _ in range(TOP_K):
            m = jnp.max(work, axis=-1, keepdims=True)
            idx = jnp.min(jnp.where(work == m, lane_f, float(LANES)), axis=-1, keepdims=True).astype(I32)
            vals.append(m)
            ids.append(idx)
            work = jnp.where(lane == idx, -3e38, work)
        es = [jnp.exp(v - vals[0]) for v in vals]
        onehots = [jnp.where(lane == idx, 1.0, 0.0) for idx in ids]
        parts.append(dict(ids=ids, es=es, tot=es[0] + es[1] + es[2] + es[3], onehots=onehots,
                          msum=onehots[0] + onehots[1] + onehots[2] + onehots[3]))
    msum = jnp.concatenate([pt["msum"] for pt in parts], axis=0)
    before = _dot(ltri_ref[...], msum.astype(BF16))
    for s, pt in enumerate(parts):
        r = slice(s * sub, (s + 1) * sub)
        rec = jnp.zeros((sub, LANES), I32)
        for k in range(TOP_K):
            rank = jnp.sum(pt["onehots"][k] * before[r], axis=-1, keepdims=True).astype(I32)
            gate_bits = lax.bitcast_convert_type(pt["es"][k] / pt["tot"], I32)
            rec = jnp.where(lane == RT_ID + k, pt["ids"][k], rec)
            rec = jnp.where(lane == RT_RANK + k, rank, rec)
            rec = jnp.where(lane == RT_GATE + k, gate_bits, rec)
        rt_ref[r, :] = rec
    cnt_ref[0] = (before[tm - 1:tm, :] + msum[tm - 1:tm, :]).astype(I32)


def _post(cfg, dn_f, dn_b, ret_f, ret_b, att, proj, xs, w_out, dn_norm, ret_norm, norm2,
          router_w, router_b):
    tm = MOE_TM
    n = cfg.t // tm
    rw = jnp.zeros((D_MODEL, LANES), F32).at[:, :N_EXPERTS].set(router_w)
    rw_hi = rw.astype(BF16)
    rw2 = jnp.stack([rw_hi, (rw - rw_hi.astype(F32)).astype(BF16)])
    rb = jnp.full((1, LANES), NEG_BIG, F32).at[0, :N_EXPERTS].set(router_b)
    ltri = (jnp.arange(tm)[:, None] > jnp.arange(tm)[None, :]).astype(BF16)
    row = lambda w: pl.BlockSpec((tm, w), lambda i: (i, 0))
    col = lambda c: pl.BlockSpec((tm, 256), lambda i: (i, c // 256))
    const = lambda *s: pl.BlockSpec(s, lambda i: (0,) * len(s))
    return pl.pallas_call(
        functools.partial(_post_kernel, cfg, len(xs)),
        grid=(n,),
        in_specs=[row(256), row(256), col(COL_DN_Z), row(256), row(256), col(COL_RG), row(512)]
                 + _group_specs(cfg, tm, D_MODEL, len(xs))
                 + [const(D_MODEL, D_MODEL), const(1, 256), const(1, 256),
                    const(1, D_MODEL), const(2, D_MODEL, LANES), const(1, LANES), const(tm, tm)],
        out_specs=[row(D_MODEL), row(D_MODEL), row(LANES), pl.BlockSpec((1, 1, LANES), lambda i: (i, 0, 0))],
        out_shape=[jax.ShapeDtypeStruct((cfg.t, D_MODEL), F32),
                   jax.ShapeDtypeStruct((cfg.t, D_MODEL), BF16),
                   jax.ShapeDtypeStruct((cfg.t, LANES), I32),
                   jax.ShapeDtypeStruct((n, 1, LANES), I32)],
        compiler_params=_cparams(),
        name="post_route",
    )(dn_f, dn_b, proj, ret_f, ret_b, proj, att, *xs, w_out.astype(BF16),
      jnp.tile(dn_norm, DN_HEADS).reshape(1, 256), ret_norm.reshape(1, 256),
      norm2.reshape(1, D_MODEL), rw2, rb, ltri)


def _route_tables(cnt, n_rows):
    n = cnt.shape[0]
    c8 = (cnt[:, :N_EXPERTS] + 7) // 8 * 8
    lstart = jnp.cumsum(c8, axis=1) - c8
    tot = jnp.sum(c8, axis=0)
    region = (tot + MOE_XB - 1) // MOE_XB * MOE_XB
    pend = jnp.cumsum(region)
    gstart = (pend - region)[None, :] + jnp.cumsum(c8, axis=0) - c8
    lrow = jnp.arange(MOE_NG) * SUBLANES
    seg_end = (lstart + c8)[:, None, :]
    e = jnp.sum((lrow[None, :, None] >= seg_end).astype(I32), axis=2)
    hot = e[:, :, None] == jnp.arange(N_EXPERTS)[None, None, :]
    g = jnp.sum(jnp.where(hot, (gstart - lstart)[:, None, :], 0), axis=2) + lrow[None, :]
    dump = n_rows - MOE_DUMP * (1 + jnp.arange(n) % 2)[:, None] + lrow[None, :]
    gdest = jnp.where(e < N_EXPERTS, g, dump).reshape(-1).astype(I32)
    n_blocks = n_rows // MOE_XB
    nvalid = (pend[-1] // MOE_XB).astype(I32)
    bstart = jnp.minimum(jnp.arange(n_blocks), nvalid - 1) * MOE_XB
    block_exp = jnp.minimum(jnp.sum((pend[None, :] <= bstart[:, None]).astype(I32), axis=1),
                            N_EXPERTS - 1).astype(I32)
    last_blk = jnp.where(region > 0, pend - MOE_XB, -1).astype(I32)
    lstart_p = jnp.zeros((n, 1, LANES), I32).at[:, 0, :N_EXPERTS].set(lstart.astype(I32))
    return lstart_p, gdest, block_exp, last_blk, nvalid.reshape(1)


def _moe_rows(cfg):
    n_tiles = cfg.t // MOE_TM
    worst = TOP_K * cfg.t + n_tiles * N_EXPERTS * 7 + N_EXPERTS * (MOE_XB - 1)
    return -(-worst // MOE_XB) * MOE_XB + 2 * MOE_DUMP


def _local_dest(rt, lstart_row):
    lane = lax.broadcasted_iota(I32, rt.shape, 1)
    ls = lstart_row.astype(F32)
    out = []
    for k in range(TOP_K):
        eid = rt[:, RT_ID + k:RT_ID + k + 1]
        base = jnp.sum(jnp.where(lane == eid, ls, 0.0), axis=-1, keepdims=True).astype(I32)
        out.append(base + rt[:, RT_RANK + k:RT_RANK + k + 1])
    return out


def _group_copy(src, dst, sem, s, d):
    return pltpu.make_async_copy(src.at[pl.ds(s, SUBLANES)], dst.at[pl.ds(d, SUBLANES)], sem)


def _dispatch_kernel(n_tiles, n_blocks, gd_ref, lb_ref, nv_ref, h2_ref, rt_ref, ls_ref, xin_ref,
                     xs_ref, sem):
    i = pl.program_id(0)
    par = i % 2

    def wait_tile(p):
        pltpu.make_async_copy(xs_ref.at[p], xin_ref.at[pl.ds(0, MOE_R)], sem.at[p]).wait()

    @pl.when(i == 0)
    def _():
        xs_ref[0, 0:MOE_XB, :] = jnp.zeros((MOE_XB, D_MODEL), F32)

        def zero_copy(d):
            return pltpu.make_async_copy(xs_ref.at[0, pl.ds(0, MOE_XB)],
                                         xin_ref.at[pl.ds(pl.multiple_of(d, MOE_XB), MOE_XB)], sem.at[0])

        def per_expert(fn):
            def body(e, c):
                d = lb_ref[e]

                @pl.when(d >= 0)
                def _():
                    fn(zero_copy(d))
                return c
            lax.fori_loop(0, N_EXPERTS, body, 0)

        def per_tail(fn):
            def body(b, c):
                fn(zero_copy(b * MOE_XB))
                return c
            lax.fori_loop(nv_ref[0], n_blocks, body, 0)

        per_expert(lambda cp: cp.start())
        per_tail(lambda cp: cp.start())
        per_expert(lambda cp: cp.wait())
        per_tail(lambda cp: cp.wait())

    @pl.when(i >= 2)
    def _():
        wait_tile(par)

    rt_t = lax.bitcast_convert_type(lax.bitcast_convert_type(rt_ref[...], F32).T, I32)
    expert = lax.broadcasted_iota(I32, (LANES, MOE_TM), 0)
    ls_row = ls_ref[0].astype(F32)
    ld = []
    for k in range(TOP_K):
        onehot = jnp.where(expert == rt_t[RT_ID + k:RT_ID + k + 1, :], 1.0, 0.0).astype(BF16)
        base = _dot_x2(ls_row, onehot)
        ld.append(base.astype(I32) + rt_t[RT_RANK + k:RT_RANK + k + 1, :])
    h2 = h2_ref[...]
    for c in range(MOE_R // MOE_RC):
        slot = c * MOE_RC + lax.broadcasted_iota(I32, (MOE_RC, MOE_TM), 0)
        hit = (slot == ld[0]) | (slot == ld[1]) | (slot == ld[2]) | (slot == ld[3])
        xs_ref[par, c * MOE_RC:(c + 1) * MOE_RC, :] = _dot(jnp.where(hit, 1.0, 0.0).astype(BF16), h2)

    def start(j, c):
        d = gd_ref[i * MOE_NG + j]
        _group_copy(xs_ref.at[par], xin_ref, sem.at[par], pl.multiple_of(j * SUBLANES, SUBLANES),
                    pl.multiple_of(d, SUBLANES)).start()
        return c

    lax.fori_loop(0, MOE_NG, start, 0, unroll=DMA_UNROLL)

    @pl.when(i == n_tiles - 1)
    def _():
        wait_tile(par)
        if n_tiles >= 2:
            wait_tile(1 - par)


def _dispatch(cfg, h2, rt, lstart, gdest, last_blk, nvalid, n_rows):
    tm = MOE_TM
    n = cfg.t // tm
    grid_spec = pltpu.PrefetchScalarGridSpec(
        num_scalar_prefetch=3,
        grid=(n,),
        in_specs=[pl.BlockSpec((tm, D_MODEL), lambda i, *_: (i, 0)),
                  pl.BlockSpec((tm, LANES), lambda i, *_: (i, 0)),
                  pl.BlockSpec((1, 1, LANES), lambda i, *_: (i, 0, 0))],
        out_specs=pl.BlockSpec(memory_space=pl.ANY),
        scratch_shapes=[pltpu.VMEM((2, MOE_R, D_MODEL), F32), pltpu.SemaphoreType.DMA((2,))],
    )
    return pl.pallas_call(
        functools.partial(_dispatch_kernel, n, n_rows // MOE_XB),
        grid_spec=grid_spec,
        out_shape=jax.ShapeDtypeStruct((n_rows, D_MODEL), F32),
        compiler_params=_cparams(),
        name="moe_dispatch",
    )(gdest, last_blk, nvalid, h2, rt, lstart)


def _expert_kernel(layer, be_ref, nv_ref, x_ref, wg_ref, bg_ref, wu_ref, bu_ref, wd_ref, bd_ref,
                   y_ref, wgc, wuc, wdc):
    del layer
    b = pl.program_id(0)
    prev = be_ref[jnp.maximum(b - 1, 0)]

    @pl.when((b == 0) | (be_ref[b] != prev))
    def _():
        wgc[...] = wg_ref[...].astype(BF16)
        wuc[...] = wu_ref[...].astype(BF16)
        wdc[...] = wd_ref[...].astype(BF16)

    @pl.when(b < nv_ref[0])
    def _():
        xb = x_ref[...].astype(BF16)
        gt = jnp.minimum(_dot(xb, wgc[...]) + bg_ref[...], SWIGLU_LIMIT)
        up = jnp.clip(_dot(xb, wuc[...]) + bu_ref[...], -SWIGLU_LIMIT, SWIGLU_LIMIT)
        hid = (up + 1.0) * gt * _sigmoid(SWIGLU_ALPHA * gt)
        y_ref[...] = _dot(hid.astype(BF16), wdc[...]) + bd_ref[...]

    @pl.when(b >= nv_ref[0])
    def _():
        y_ref[...] = jnp.zeros_like(y_ref)


def _experts(layer, xin, block_exp, nvalid, w_gate, b_gate, w_up, b_up, w_down, b_down):
    n_blocks = xin.shape[0] // MOE_XB
    d = D_MODEL
    nl, ne = w_gate.shape[0], w_gate.shape[1]
    xmap = lambda b, be, nv: (jnp.minimum(b, nv[0] - 1), 0)
    wmap = lambda b, be, nv: (layer, be[b], 0, 0)
    wspec = pl.BlockSpec((None, None, d, d), wmap)
    bspec = pl.BlockSpec((None, None, 1, d), wmap)
    grid_spec = pltpu.PrefetchScalarGridSpec(
        num_scalar_prefetch=2,
        grid=(n_blocks,),
        in_specs=[pl.BlockSpec((MOE_XB, d), xmap), wspec, bspec, wspec, bspec, wspec, bspec],
        out_specs=pl.BlockSpec((MOE_XB, d), lambda b, be, nv: (b, 0)),
        scratch_shapes=[pltpu.VMEM((d, d), BF16)] * 3,
    )
    return pl.pallas_call(
        functools.partial(_expert_kernel, layer),
        grid_spec=grid_spec,
        out_shape=jax.ShapeDtypeStruct(xin.shape, F32),
        compiler_params=_cparams(),
        name="moe_experts",
    )(block_exp, nvalid, xin, w_gate, b_gate.reshape(nl, ne, 1, d), w_up, b_up.reshape(nl, ne, 1, d),
      w_down, b_down.reshape(nl, ne, 1, d))


def _combine_kernel(cfg, final, gd_ref, rt_ref, ls_ref, x1_ref, pa_ref, pb_ref, pg_ref, pp_ref,
                    pn_ref, fn_ref, y_ref, *refs):
    o_refs, (ys_ref, sem) = refs[:-2], refs[-2:]
    n_tiles = cfg.t // MOE_TM
    i = pl.program_id(0)
    par = i % 2

    def fetch(tile, p):
        def start(j, c):
            d = gd_ref[tile * MOE_NG + j]
            _group_copy(y_ref, ys_ref.at[p], sem.at[p], pl.multiple_of(d, SUBLANES),
                        pl.multiple_of(j * SUBLANES, SUBLANES)).start()
            return c
        lax.fori_loop(0, MOE_NG, start, 0, unroll=DMA_UNROLL)

    @pl.when(i == 0)
    def _():
        fetch(0, 0)

    @pl.when(i + 1 < n_tiles)
    def _():
        fetch(i + 1, 1 - par)

    rt = rt_ref[...]
    ld = _local_dest(rt, ls_ref[0])
    gates = [lax.bitcast_convert_type(rt[:, RT_GATE + k:RT_GATE + k + 1], F32) for k in range(TOP_K)]
    pltpu.make_async_copy(y_ref.at[pl.ds(0, MOE_R)], ys_ref.at[par], sem.at[par]).wait()
    x2 = x1_ref[...]
    for c in range(MOE_R // MOE_RC):
        slot = c * MOE_RC + lax.broadcasted_iota(I32, (MOE_TM, MOE_RC), 1)
        ptg = jnp.zeros((MOE_TM, MOE_RC), F32)
        for k in range(TOP_K):
            ptg = jnp.where(slot == ld[k], gates[k], ptg)
        x2 = x2 + _dot(ptg.astype(BF16), ys_ref[par, c * MOE_RC:(c + 1) * MOE_RC, :].astype(BF16))
    gate = _sigmoid(_dot(x2.astype(BF16), pg_ref[...]))
    p = _group_load(cfg, MOE_TM, (pa_ref, pb_ref))
    e = _dot(p.astype(BF16), pp_ref[...]) * gate
    x3 = x2 + _rms(e, pn_ref[...])
    if not final:
        o_refs[0][...] = x3
    else:
        x3 = _rms(x3, fn_ref[...])
        is_prompt = i < cfg.t0 // MOE_TM

        @pl.when(is_prompt)
        def _():
            o_refs[0][...] = x3

        @pl.when(jnp.logical_not(is_prompt))
        def _():
            o_refs[1][...] = x3


def _combine(cfg, layer, final, rt, lstart, gdest, x1, p_prompt, p_sample, ple_gate, ple_proj,
             ple_norm, final_norm, y):
    tm = MOE_TM
    n = cfg.t // tm
    row = lambda w: pl.BlockSpec((tm, w), lambda i, gd: (i, 0))
    const = lambda *s: pl.BlockSpec(s, lambda i, gd: (0,) * len(s))
    if final:
        out_specs = _group_specs(cfg, tm, D_MODEL, 2)
        out_shape = [jax.ShapeDtypeStruct((cfg.t0, D_MODEL), F32),
                     jax.ShapeDtypeStruct((cfg.t1, D_MODEL), F32)]
    else:
        out_specs = [row(D_MODEL)]
        out_shape = [jax.ShapeDtypeStruct((cfg.t, D_MODEL), F32)]
    grid_spec = pltpu.PrefetchScalarGridSpec(
        num_scalar_prefetch=1,
        grid=(n,),
        in_specs=[row(LANES), pl.BlockSpec((1, 1, LANES), lambda i, gd: (i, 0, 0)), row(D_MODEL)]
                 + _group_specs(cfg, tm, PLE_DIM, 2, lead=layer)
                 + [const(D_MODEL, D_MODEL), const(PLE_DIM, D_MODEL),
                    const(1, D_MODEL), const(1, D_MODEL), pl.BlockSpec(memory_space=pl.ANY)],
        out_specs=out_specs,
        scratch_shapes=[pltpu.VMEM((2, MOE_R, D_MODEL), F32), pltpu.SemaphoreType.DMA((2,))],
    )
    outs = pl.pallas_call(
        functools.partial(_combine_kernel, cfg, final),
        grid_spec=grid_spec,
        out_shape=out_shape,
        compiler_params=_cparams(),
        name="moe_combine",
    )(gdest, rt, lstart, x1, p_prompt, p_sample, ple_gate.astype(BF16), ple_proj.astype(BF16),
      ple_norm.reshape(1, D_MODEL), final_norm.reshape(1, D_MODEL), y)
    return outs if final else outs[0]


def _layer(cfg, layer, final, x, p_prompt, p_sample, tabs, norm1, w_in, conv_w, dn_a_log,
           dn_dt_bias, dn_norm, ret_decay, ret_norm, w_out, norm2, router_w, router_b, w_gate,
           b_gate, w_up, b_up, w_down, b_down, ple_proj, ple_gate, ple_norm, final_norm):
    proj = _in_proj(cfg, x, norm1[layer], _reorder_w_in(w_in[layer]), tabs[0], tabs[1], PROJ_TM)
    qkv, gates = _dn_prep(cfg, proj, conv_w[layer], dn_a_log[layer], dn_dt_bias[layer], PROJ_TM)
    dn_f, dn_b = _dn_scan(cfg, qkv, gates, SCAN_TM)
    ret_f, ret_b = _ret_scan(cfg, proj, _ret_tables(ret_decay[layer]), SCAN_TM)
    att = _attention(cfg, proj)
    x1, h2, rt, cnt = _post(cfg, dn_f, dn_b, ret_f, ret_b, att, proj, x, w_out[layer],
                            dn_norm[layer], ret_norm[layer], norm2[layer], router_w[layer],
                            router_b[layer])
    n_rows = _moe_rows(cfg)
    lstart, gdest, block_exp, last_blk, nvalid = _route_tables(cnt[:, 0, :], n_rows)
    xin = _dispatch(cfg, h2, rt, lstart, gdest, last_blk, nvalid, n_rows)
    y = _experts(layer, xin, block_exp, nvalid, w_gate, b_gate, w_up, b_up, w_down, b_down)
    return _combine(cfg, layer, final, rt, lstart, gdest, x1, p_prompt, p_sample, ple_gate[layer],
                    ple_proj[layer], ple_norm[layer], final_norm, y)


def _trunk(cfg, x_parts, p_prompt, p_sample, final_norm, params):
    tabs = _rope_tables(cfg.smax)
    depth = p_prompt.shape[0]
    x = x_parts
    for layer in range(depth):
        final = layer == depth - 1
        out = _layer(cfg, layer, final, x, p_prompt, p_sample, tabs, *params, final_norm)
        x = out if final else (out,)
    return x


def kernel(x_prompt, x_sample, p_prompt, p_sample, norm1, w_in, conv_w, dn_a_log, dn_dt_bias,
           dn_norm, ret_decay, ret_norm, w_out, norm2, router_w, router_b, w_gate, b_gate,
           w_up, b_up, w_down, b_down, ple_proj, ple_gate, ple_norm, final_norm):
    b0, s0, d = x_prompt.shape
    b1, s1, _ = x_sample.shape
    cfg = Cfg(b0, s0, b1, s1)
    depth = p_prompt.shape[0]
    params = (norm1, w_in, conv_w, dn_a_log, dn_dt_bias, dn_norm, ret_decay, ret_norm, w_out,
              norm2, router_w, router_b, w_gate, b_gate, w_up, b_up, w_down, b_down,
              ple_proj, ple_gate, ple_norm)
    y0, y1 = _trunk(cfg, (x_prompt.reshape(cfg.t0, d), x_sample.reshape(cfg.t1, d)),
                    p_prompt.reshape(depth, cfg.t0, PLE_DIM), p_sample.reshape(depth, cfg.t1, PLE_DIM),
                    final_norm, params)
    return (y0.reshape(b0, s0, d), y1.reshape(b1, s1, d))
```
